```python
import math
import jax, jax.numpy as jnp
from jax import lax
import numpy as np

D_MODEL = 2048
BATCH = 1
SEQ = 8192
DEPTH = 1
DEC_BATCH = 1
DEC_SEQ = 16384
PAST_LEN = 128

DA_HEADS = 8
DA_QK_DIM = 64
DA_V_DIM = 2 * DA_QK_DIM
DA_ROT_DIM = DA_QK_DIM // 4
MLA_HEADS = 8
MLA_Q_RANK = 512
MLA_KV_RANK = 512
MLA_NOPE_DIM = 128
MLA_ROPE_DIM = 64
MLA_V_DIM = 128
ROPE_THETA = 500000.0
Q_BLOCK = 128
NORM_EPS = 1e-6
FFN_HIDDEN = -(-8 * D_MODEL // (3 * 256)) * 256
DA_Q_COLS = DA_HEADS * 2 * DA_QK_DIM
DA_K_COLS = DA_HEADS * 2 * DA_QK_DIM
DA_V_COLS = DA_HEADS * DA_V_DIM
GATE_COLS = 2 * D_MODEL
IN_COLS = DA_Q_COLS + DA_K_COLS + DA_V_COLS + MLA_Q_RANK + MLA_KV_RANK + MLA_ROPE_DIM + GATE_COLS
DA_OUT = DA_HEADS * DA_V_DIM
MLA_OUT = MLA_HEADS * MLA_V_DIM

kernel_name = "hybrid_diffattn_mla_gated_encoder"


def rmsnorm(x, g):
    xf = x.astype(jnp.float32)
    y = xf * lax.rsqrt(jnp.mean(xf * xf, axis=-1, keepdims=True) + NORM_EPS)
    return (y * g.astype(jnp.float32)).astype(x.dtype)


def rope(x, rot_dim):
    s = x.shape[1]
    pos = jnp.arange(s, dtype=jnp.float32)
    inv_freq = ROPE_THETA ** (-jnp.arange(0, rot_dim, 2, dtype=jnp.float32) / rot_dim)
    ang = pos[:, None] * inv_freq[None, :]
    ang = ang.reshape((s,) + (1,) * (x.ndim - 3) + (rot_dim // 2,))
    cos, sin = jnp.cos(ang), jnp.sin(ang)
    xr = x[..., :rot_dim].astype(jnp.float32)
    x1, x2 = xr[..., : rot_dim // 2], xr[..., rot_dim // 2:]
    rot = jnp.concatenate([x1 * cos - x2 * sin, x2 * cos + x1 * sin], axis=-1).astype(x.dtype)
    return jnp.concatenate([rot, x[..., rot_dim:]], axis=-1)


def to_blocks(t):
    b, s = t.shape[:2]
    return jnp.moveaxis(t.reshape((b, s // Q_BLOCK, Q_BLOCK) + t.shape[2:]), 1, 0)


def from_blocks(t):
    t = jnp.moveaxis(t, 0, 1)
    return t.reshape((t.shape[0], t.shape[1] * t.shape[2]) + t.shape[3:])


def softmax_f32(s, scale):
    return jax.nn.softmax(s.astype(jnp.float32) * scale, axis=-1)


def diff_attention(q, k, v, lam, subln_g, lambda_init):
    b, s = q.shape[:2]
    scale = DA_QK_DIM ** -0.5
    k1, k2 = k[..., 0, :], k[..., 1, :]

    def block(qb):
        a1 = softmax_f32(jnp.einsum('bqhd,bkhd->bhqk', qb[..., 0, :], k1), scale)
        a2 = softmax_f32(jnp.einsum('bqhd,bkhd->bhqk', qb[..., 1, :], k2), scale)
        a = (a1 - lam * a2).astype(v.dtype)
        return jnp.einsum('bhqk,bkhd->bqhd', a, v)

    o = from_blocks(lax.map(block, to_blocks(q)))
    o = rmsnorm(o, subln_g) * (1.0 - lambda_init)
    return o.reshape(b, s, DA_OUT)


def mla_attention(c_q, c_kv, k_rope, q_norm_g, w_q_b, kv_norm_g, w_kv_b):
    b, s = c_q.shape[:2]
    q = (rmsnorm(c_q, q_norm_g) @ w_q_b).reshape(b, s, MLA_HEADS, MLA_NOPE_DIM + MLA_ROPE_DIM)
    q_nope, q_rope = q[..., :MLA_NOPE_DIM], rope(q[..., MLA_NOPE_DIM:], MLA_ROPE_DIM)
    kv = (rmsnorm(c_kv, kv_norm_g) @ w_kv_b).reshape(b, s, MLA_HEADS, MLA_NOPE_DIM + MLA_V_DIM)
    k_nope, v = kv[..., :MLA_NOPE_DIM], kv[..., MLA_NOPE_DIM:]
    k_r = rope(k_rope[:, :, None, :], MLA_ROPE_DIM)[:, :, 0, :]
    scale = (MLA_NOPE_DIM + MLA_ROPE_DIM) ** -0.5

    def block(qs):
        qn, qr = qs
        sc = jnp.einsum('bqhd,bkhd->bhqk', qn, k_nope) + jnp.einsum('bqhd,bkd->bhqk', qr, k_r)
        p = softmax_f32(sc, scale).astype(v.dtype)
        return jnp.einsum('bhqk,bkhd->bqhd', p, v)

    o = from_blocks(lax.map(block, (to_blocks(q_nope), to_blocks(q_rope))))
    return o.reshape(b, s, MLA_OUT)


def encoder_layer(x, lambda_init, attn_norm_g, w_in, da_lambda_q1, da_lambda_k1, da_lambda_q2,
                  da_lambda_k2, da_subln_g, mla_q_norm_g, mla_w_q_b, mla_kv_norm_g, mla_w_kv_b,
                  w_branch_da, w_branch_mla, w_out, ffn_norm_g, w_gate, w_up, w_down):
    b, s, _ = x.shape
    h = rmsnorm(x, attn_norm_g)
    proj = h @ w_in
    cuts = np.cumsum([DA_Q_COLS, DA_K_COLS, DA_V_COLS, MLA_Q_RANK, MLA_KV_RANK, MLA_ROPE_DIM]).tolist()
    q_da, k_da, v_da, c_q, c_kv, k_rope, gate_logits = jnp.split(proj, cuts, axis=-1)
    q_da = rope(q_da.reshape(b, s, DA_HEADS, 2, DA_QK_DIM), DA_ROT_DIM)
    k_da = rope(k_da.reshape(b, s, DA_HEADS, 2, DA_QK_DIM), DA_ROT_DIM)
    v_da = v_da.reshape(b, s, DA_HEADS, DA_V_DIM)
    lam = (jnp.exp(jnp.sum(da_lambda_q1.astype(jnp.float32) * da_lambda_k1.astype(jnp.float32)))
           - jnp.exp(jnp.sum(da_lambda_q2.astype(jnp.float32) * da_lambda_k2.astype(jnp.float32)))
           + lambda_init)
    o_da = diff_attention(q_da, k_da, v_da, lam, da_subln_g, lambda_init)
    o_mla = mla_attention(c_q, c_kv, k_rope, mla_q_norm_g, mla_w_q_b, mla_kv_norm_g, mla_w_kv_b)
    gates = jax.nn.sigmoid(gate_logits.astype(jnp.float32)).astype(x.dtype)
    g_da, g_mla = gates[..., :D_MODEL], gates[..., D_MODEL:]
    merged = g_da * (o_da @ w_branch_da) + g_mla * (o_mla @ w_branch_mla)
    x = x + merged @ w_out
    h = rmsnorm(x, ffn_norm_g)
    x = x + (jax.nn.silu(h @ w_gate) * (h @ w_up)) @ w_down
    return x


def trunk(x, attn_norm_g, w_in, da_lambda_q1, da_lambda_k1, da_lambda_q2, da_lambda_k2, da_subln_g,
          mla_q_norm_g, mla_w_q_b, mla_kv_norm_g, mla_w_kv_b, w_branch_da, w_branch_mla, w_out,
          ffn_norm_g, w_gate, w_up, w_down, final_norm_g):
    for l in range(DEPTH):
        lambda_init = 0.8 - 0.6 * math.exp(-0.3 * l)
        x = encoder_layer(x, lambda_init, attn_norm_g[l], w_in[l], da_lambda_q1[l], da_lambda_k1[l],
                          da_lambda_q2[l], da_lambda_k2[l], da_subln_g[l], mla_q_norm_g[l], mla_w_q_b[l],
                          mla_kv_norm_g[l], mla_w_kv_b[l], w_branch_da[l], w_branch_mla[l], w_out[l],
                          ffn_norm_g[l], w_gate[l], w_up[l], w_down[l])
    return rmsnorm(x, final_norm_g)


def setup_inputs(seed: int = 0) -> dict:
    key = jax.random.key(seed)
    ks = jax.random.split(key, 24)

    def w(k, shape, fan_in):
        return jax.random.normal(k, shape, jnp.float32) * fan_in ** -0.5

    def gain(k, shape):
        return 1.0 + 0.02 * jax.random.normal(k, shape, jnp.float32)

    def lam(k):
        return 0.1 * jax.random.normal(k, (DEPTH, DA_QK_DIM), jnp.float32)

    return {
        "x_prompt": jax.random.normal(ks[0], (BATCH, SEQ, D_MODEL), jnp.float32),
        "x_sample": jax.random.normal(ks[1], (DEC_BATCH, DEC_SEQ, D_MODEL), jnp.float32),
        "attn_norm_g": gain(ks[2], (DEPTH, D_MODEL)),
        "w_in": w(ks[3], (DEPTH, D_MODEL, IN_COLS), D_MODEL),
        "da_lambda_q1": lam(ks[4]),
        "da_lambda_k1": lam(ks[5]),
        "da_lambda_q2": lam(ks[6]),
        "da_lambda_k2": lam(ks[7]),
        "da_subln_g": gain(ks[8], (DEPTH, DA_V_DIM)),
        "mla_q_norm_g": gain(ks[9], (DEPTH, MLA_Q_RANK)),
        "mla_w_q_b": w(ks[10], (DEPTH, MLA_Q_RANK, MLA_HEADS * (MLA_NOPE_DIM + MLA_ROPE_DIM)), MLA_Q_RANK),
        "mla_kv_norm_g": gain(ks[11], (DEPTH, MLA_KV_RANK)),
        "mla_w_kv_b": w(ks[12], (DEPTH, MLA_KV_RANK, MLA_HEADS * (MLA_NOPE_DIM + MLA_V_DIM)), MLA_KV_RANK),
        "w_branch_da": w(ks[13], (DEPTH, DA_OUT, D_MODEL), DA_OUT),
        "w_branch_mla": w(ks[14], (DEPTH, MLA_OUT, D_MODEL), MLA_OUT),
        "w_out": w(ks[15], (DEPTH, D_MODEL, D_MODEL), D_MODEL),
        "ffn_norm_g": gain(ks[16], (DEPTH, D_MODEL)),
        "w_gate": w(ks[17], (DEPTH, D_MODEL, FFN_HIDDEN), D_MODEL),
        "w_up": w(ks[18], (DEPTH, D_MODEL, FFN_HIDDEN), D_MODEL),
        "w_down": w(ks[19], (DEPTH, FFN_HIDDEN, D_MODEL), FFN_HIDDEN),
        "final_norm_g": gain(ks[20], (D_MODEL,)),
    }


def reference(x_prompt, x_sample, attn_norm_g, w_in, da_lambda_q1, da_lambda_k1, da_lambda_q2,
              da_lambda_k2, da_subln_g, mla_q_norm_g, mla_w_q_b, mla_kv_norm_g, mla_w_kv_b,
              w_branch_da, w_branch_mla, w_out, ffn_norm_g, w_gate, w_up, w_down, final_norm_g):
    params = (attn_norm_g, w_in, da_lambda_q1, da_lambda_k1, da_lambda_q2, da_lambda_k2, da_subln_g,
              mla_q_norm_g, mla_w_q_b, mla_kv_norm_g, mla_w_kv_b, w_branch_da, w_branch_mla, w_out,
              ffn_norm_g, w_gate, w_up, w_down, final_norm_g)
    y_prompt = trunk(x_prompt, *params)
    y_sample = trunk(x_sample, *params)
    return (y_prompt, y_sample)
```

```python
import functools
import math

import jax
import jax.numpy as jnp
from jax import lax
from jax.experimental import pallas as pl
from jax.experimental.pallas import tpu as pltpu

D_MODEL = 2048
DA_HEADS = 8
DA_QK_DIM = 64
DA_V_DIM = 2 * DA_QK_DIM
DA_ROT_DIM = DA_QK_DIM // 4
MLA_HEADS = 8
MLA_Q_RANK = 512
MLA_KV_RANK = 512
MLA_NOPE_DIM = 128
MLA_ROPE_DIM = 64
MLA_V_DIM = 128
ROPE_THETA = 500000.0
NORM_EPS = 1e-6
FFN_HIDDEN = -(-8 * D_MODEL // (3 * 256)) * 256
DA_COLS = DA_HEADS * 2 * DA_QK_DIM
MLA_IN_COLS = MLA_Q_RANK + MLA_KV_RANK + MLA_ROPE_DIM
GATE_COLS = 2 * D_MODEL

LANES = 128
MLA_HEAD_BLOCK = 2 * LANES
VMEM_LIMIT_BYTES = 56 * 1024 * 1024
LOG2E = math.log2(math.e)
DA_Q_SCALE = DA_QK_DIM ** -0.5 * LOG2E
MLA_Q_SCALE = (MLA_NOPE_DIM + MLA_ROPE_DIM) ** -0.5 * LOG2E

BF16 = jnp.bfloat16
F32 = jnp.float32
_NT = (((1,), (1,)), ((), ()))


def _params(n_grid_dims):
    return pltpu.CompilerParams(
        dimension_semantics=("arbitrary",) * n_grid_dims,
        vmem_limit_bytes=VMEM_LIMIT_BYTES,
    )


def _rms(x, g):
    var = jnp.mean(x * x, axis=-1, keepdims=True)
    return x * lax.rsqrt(var + NORM_EPS) * g


def _rope_tables(seq, rot_dim, period):
    half = rot_dim // 2
    pos = jnp.arange(seq, dtype=F32)
    inv_freq = ROPE_THETA ** (-jnp.arange(0, rot_dim, 2, dtype=F32) / rot_dim)
    ang = pos[:, None] * inv_freq[None, :]
    cos, sin = jnp.cos(ang), jnp.sin(ang)
    rest = period - rot_dim
    zeros_half = jnp.zeros((seq, half), F32)
    zeros_rest = jnp.zeros((seq, rest), F32)
    a = jnp.concatenate([cos, cos, jnp.ones((seq, rest), F32)], axis=-1)
    b = jnp.concatenate([-sin, zeros_half, zeros_rest], axis=-1)
    c = jnp.concatenate([zeros_half, sin, zeros_rest], axis=-1)
    reps = LANES // period
    return tuple(jnp.tile(t, (1, reps)) for t in (a, b, c))


def _rope_tile(x, a, b, c, half):
    return x * a + pltpu.roll(x, LANES - half, 1) * b + pltpu.roll(x, half, 1) * c


def _rmsnorm_kernel(x_ref, g_ref, o_ref):
    o_ref[...] = _rms(x_ref[...], g_ref[...]).astype(o_ref.dtype)


def _rmsnorm_bf16(x, g, tm=512):
    s, d = x.shape
    return pl.pallas_call(
        _rmsnorm_kernel,
        grid=(s // tm,),
        in_specs=[pl.BlockSpec((tm, d), lambda i: (i, 0)), pl.BlockSpec((1, d), lambda i: (0, 0))],
        out_specs=pl.BlockSpec((tm, d), lambda i: (i, 0)),
        out_shape=jax.ShapeDtypeStruct((s, d), BF16),
        compiler_params=_params(1),
        name="rmsnorm",
    )(x, g.reshape(1, d))


def _proj_kernel(*refs, mode):
    if mode in ("rope_q", "rope_k"):
        h_ref, w_ref, a_ref, b_ref, c_ref, o_ref = refs
    else:
        h_ref, w_ref, o_ref = refs
    acc = jnp.dot(h_ref[...], w_ref[...], preferred_element_type=F32)
    if mode == "plain":
        o_ref[...] = acc.astype(o_ref.dtype)
        return
    if mode == "sigmoid":
        o_ref[...] = jax.nn.sigmoid(acc)
        return
    a, b, c = a_ref[...], b_ref[...], c_ref[...]
    tm, tn = acc.shape
    first_component = lax.broadcasted_iota(jnp.int32, (tm, LANES), 1) < DA_QK_DIM
    for t in range(tn // LANES):
        cols = slice(t * LANES, (t + 1) * LANES)
        y = _rope_tile(acc[:, cols], a, b, c, DA_ROT_DIM // 2)
        if mode == "rope_k":
            o_ref[:, cols] = y.astype(o_ref.dtype)
        else:
            y = y * DA_Q_SCALE
            o_ref[0, :, cols] = jnp.where(first_component, y, 0.0).astype(o_ref.dtype)
            o_ref[1, :, cols] = jnp.where(first_component, 0.0, y).astype(o_ref.dtype)


def _proj(h, w, mode, tables=(), tm=512, tn=512):
    s, k = h.shape
    n = w.shape[1]
    in_specs = [pl.BlockSpec((tm, k), lambda i, j: (i, 0)), pl.BlockSpec((k, tn), lambda i, j: (0, j))]
    in_specs += [pl.BlockSpec((tm, LANES), lambda i, j: (i, 0)) for _ in tables]
    if mode == "rope_q":
        out_shape = jax.ShapeDtypeStruct((2, s, n), BF16)
        out_spec = pl.BlockSpec((2, tm, tn), lambda i, j: (0, i, j))
    else:
        out_shape = jax.ShapeDtypeStruct((s, n), F32 if mode == "sigmoid" else BF16)
        out_spec = pl.BlockSpec((tm, tn), lambda i, j: (i, j))
    return pl.pallas_call(
        functools.partial(_proj_kernel, mode=mode),
        grid=(s // tm, n // tn),
        in_specs=in_specs,
        out_specs=out_spec,
        out_shape=out_shape,
        compiler_params=_params(2),
        name="proj_" + mode,
    )(h, w, *tables)


def _mla_pre_kernel(h_ref, win_ref, gq_ref, wqb_ref, gkv_ref, wkvb_ref, a_ref, b_ref, c_ref,
                    q_ref, k_ref, v_ref):
    a, b, c = a_ref[...], b_ref[...], c_ref[...]
    half = MLA_ROPE_DIM // 2
    lat = jnp.dot(h_ref[...], win_ref[...], preferred_element_type=F32)
    cq = _rms(lat[:, :MLA_Q_RANK], gq_ref[...]).astype(BF16)
    ckv = _rms(lat[:, MLA_Q_RANK:MLA_Q_RANK + MLA_KV_RANK], gkv_ref[...]).astype(BF16)
    k_rope = _rope_tile(lat[:, MLA_Q_RANK + MLA_KV_RANK:], a, b, c, half).astype(BF16)
    q = jnp.dot(cq, wqb_ref[...], preferred_element_type=F32)
    kv = jnp.dot(ckv, wkvb_ref[...], preferred_element_type=F32)
    for hd in range(MLA_HEADS):
        lo = hd * MLA_HEAD_BLOCK
        mid = lo + LANES
        hi = lo + MLA_HEAD_BLOCK
        q_ref[:, lo:mid] = (q[:, lo:mid] * MLA_Q_SCALE).astype(BF16)
        q_ref[:, mid:hi] = (_rope_tile(q[:, mid:hi], a, b, c, half) * MLA_Q_SCALE).astype(BF16)
        k_ref[:, lo:mid] = kv[:, lo:mid].astype(BF16)
        k_ref[:, mid:hi] = k_rope
        v_ref[:, hd * LANES:(hd + 1) * LANES] = kv[:, mid:hi].astype(BF16)


def _mla_pre(h, w_mla_in, gq, wqb, gkv, wkvb, tables, tm=256):
    s, d = h.shape
    qk_cols = MLA_HEADS * MLA_HEAD_BLOCK
    v_cols = MLA_HEADS * MLA_V_DIM
    full = lambda arr: pl.BlockSpec(arr.shape, lambda i: (0, 0))
    row = lambda n: pl.BlockSpec((tm, n), lambda i: (i, 0))
    return pl.pallas_call(
        _mla_pre_kernel,
        grid=(s // tm,),
        in_specs=[row(d), full(w_mla_in), full(gq), full(wqb), full(gkv), full(wkvb),
                  row(LANES), row(LANES), row(LANES)],
        out_specs=[row(qk_cols), row(qk_cols), row(v_cols)],
        out_shape=[jax.ShapeDtypeStruct((s, qk_cols), BF16),
                   jax.ShapeDtypeStruct((s, qk_cols), BF16),
                   jax.ShapeDtypeStruct((s, v_cols), BF16)],
        compiler_params=_params(1),
        name="mla_pre",
    )(h, w_mla_in, gq, wqb, gkv, wkvb, *tables)


def _flash_kernel(*refs, mode, tk, lambda_init):
    if mode == "da":
        q_ref, k_ref, v_ref, lq1_ref, lk1_ref, lq2_ref, lk2_ref, g_ref, o_ref, m_sc, l_sc, acc_sc = refs
    else:
        q_ref, k_ref, v_ref, o_ref, m_sc, l_sc, acc_sc = refs
    rows = m_sc.shape[0]
    q = q_ref[...].reshape(rows, q_ref.shape[-1])
    m_sc[...] = jnp.full(m_sc.shape, -jnp.inf, F32)
    l_sc[...] = jnp.zeros(l_sc.shape, F32)
    acc_sc[...] = jnp.zeros(acc_sc.shape, F32)

    def kv_step(j, carry):
        off = pl.multiple_of(j * tk, tk)
        k = k_ref[pl.ds(off, tk), :]
        v = v_ref[pl.ds(off, tk), :]
        s = lax.dot_general(q, k, _NT, preferred_element_type=F32)
        m_prev = m_sc[...]
        m_new = jnp.maximum(m_prev, jnp.max(s, axis=-1, keepdims=True))
        alpha = jnp.exp2(m_prev - m_new)
        p = jnp.exp2(s - m_new)
        l_sc[...] = alpha * l_sc[...] + jnp.sum(p, axis=-1, keepdims=True)
        acc_sc[...] = alpha * acc_sc[...] + jnp.dot(p.astype(BF16), v, preferred_element_type=F32)
        m_sc[...] = m_new
        return carry

    lax.fori_loop(0, k_ref.shape[0] // tk, kv_step, 0)

    o = acc_sc[...] / l_sc[...]
    if mode == "da":
        tq = rows // 2
        lam = (jnp.exp(jnp.sum(lq1_ref[...] * lk1_ref[...])) - jnp.exp(jnp.sum(lq2_ref[...] * lk2_ref[...]))
               + lambda_init)
        diff = o[:tq] - lam * o[tq:]
        o = _rms(diff, g_ref[...]) * (1.0 - lambda_init)
    o_ref[...] = o.astype(o_ref.dtype)


def _flash(q, k, v, mode, extra=(), lambda_init=0.0, tq=512, tk=512):
    s = k.shape[0]
    heads = v.shape[1] // LANES
    dk = k.shape[1] // heads
    if mode == "da":
        rows = 2 * tq
        q_spec = pl.BlockSpec((2, tq, dk), lambda h, i: (0, i, h))
    else:
        rows = tq
        q_spec = pl.BlockSpec((tq, dk), lambda h, i: (i, h))
    in_specs = [q_spec,
                pl.BlockSpec((s, dk), lambda h, i: (0, h)),
                pl.BlockSpec((s, LANES), lambda h, i: (0, h))]
    in_specs += [pl.BlockSpec(e.shape, lambda h, i: (0, 0)) for e in extra]
    return pl.pallas_call(
        functools.partial(_flash_kernel, mode=mode, tk=tk, lambda_init=lambda_init),
        grid=(heads, s // tq),
        in_specs=in_specs,
        out_specs=pl.BlockSpec((tq, LANES), lambda h, i: (i, h)),
        out_shape=jax.ShapeDtypeStruct((s, heads * LANES), BF16),
        scratch_shapes=[pltpu.VMEM((rows, 1), F32), pltpu.VMEM((rows, 1), F32),
                        pltpu.VMEM((rows, LANES), F32)],
        compiler_params=_params(2),
        name="flash_" + mode,
    )(q, k, v, *extra)


def _merge_kernel(oda_ref, omla_ref, gda_ref, gmla_ref, wda_ref, wmla_ref, wo_ref, x_ref, o_ref, acc_sc):
    c = pl.program_id(1)

    @pl.when(c == 0)
    def _():
        acc_sc[...] = jnp.zeros(acc_sc.shape, F32)

    b_da = jnp.dot(oda_ref[...], wda_ref[...], preferred_element_type=F32)
    b_mla = jnp.dot(omla_ref[...], wmla_ref[...], preferred_element_type=F32)
    merged = gda_ref[...] * b_da + gmla_ref[...] * b_mla
    acc_sc[...] += jnp.dot(merged.astype(BF16), wo_ref[...], preferred_element_type=F32)

    @pl.when(c == pl.num_programs(1) - 1)
    def _():
        o_ref[...] = x_ref[...] + acc_sc[...]


def _merge(o_da, o_mla, gates, w_da, w_mla, w_out, x, tm=512, tc=512):
    s, d = x.shape
    n_c = d // tc
    return pl.pallas_call(
        _merge_kernel,
        grid=(s // tm, n_c),
        in_specs=[pl.BlockSpec((tm, o_da.shape[1]), lambda i, c: (i, 0)),
                  pl.BlockSpec((tm, o_mla.shape[1]), lambda i, c: (i, 0)),
                  pl.BlockSpec((tm, tc), lambda i, c: (i, c)),
                  pl.BlockSpec((tm, tc), lambda i, c: (i, c + n_c)),
                  pl.BlockSpec((w_da.shape[0], tc), lambda i, c: (0, c)),
                  pl.BlockSpec((w_mla.shape[0], tc), lambda i, c: (0, c)),
                  pl.BlockSpec((tc, d), lambda i, c: (c, 0)),
                  pl.BlockSpec((tm, d), lambda i, c: (i, 0))],
        out_specs=pl.BlockSpec((tm, d), lambda i, c: (i, 0)),
        out_shape=jax.ShapeDtypeStruct((s, d), F32),
        scratch_shapes=[pltpu.VMEM((tm, d), F32)],
        compiler_params=_params(2),
        name="merge",
    )(o_da, o_mla, gates, gates, w_da, w_mla, w_out, x)


def _ffn_kernel(x_ref, g_ref, wg_ref, wu_ref, wd_ref, gf_ref, o_ref, h_sc, acc_sc, *, final_norm):
    f = pl.program_id(1)

    @pl.when(f == 0)
    def _():
        h_sc[...] = _rms(x_ref[...], g_ref[...]).astype(h_sc.dtype)
        acc_sc[...] = jnp.zeros(acc_sc.shape, F32)

    h = h_sc[...]
    gate = jnp.dot(h, wg_ref[...], preferred_element_type=F32)
    up = jnp.dot(h, wu_ref[...], preferred_element_type=F32)
    act = (jax.nn.silu(gate) * up).astype(BF16)
    acc_sc[...] += jnp.dot(act, wd_ref[...], preferred_element_type=F32)

    @pl.when(f == pl.num_programs(1) - 1)
    def _():
        y = x_ref[...] + acc_sc[...]
        o_ref[...] = _rms(y, gf_ref[...]) if final_norm else y


def _ffn(x, g, w_gate, w_up, w_down, g_final, final_norm, tm=512, tf=512):
    s, d = x.shape
    hidden = w_gate.shape[1]
    return pl.pallas_call(
        functools.partial(_ffn_kernel, final_norm=final_norm),
        grid=(s // tm, hidden // tf),
        in_specs=[pl.BlockSpec((tm, d), lambda i, f: (i, 0)),
                  pl.BlockSpec((1, d), lambda i, f: (0, 0)),
                  pl.BlockSpec((d, tf), lambda i, f: (0, f)),
                  pl.BlockSpec((d, tf), lambda i, f: (0, f)),
                  pl.BlockSpec((tf, d), lambda i, f: (f, 0)),
                  pl.BlockSpec((1, d), lambda i, f: (0, 0))],
        out_specs=pl.BlockSpec((tm, d), lambda i, f: (i, 0)),
        out_shape=jax.ShapeDtypeStruct((s, d), F32),
        scratch_shapes=[pltpu.VMEM((tm, d), BF16), pltpu.VMEM((tm, d), F32)],
        compiler_params=_params(2),
        name="ffn",
    )(x, g.reshape(1, d), w_gate, w_up, w_down, g_final.reshape(1, d))


def _prepare_layer(l, attn_norm_g, w_in, da_lambda_q1, da_lambda_k1, da_lambda_q2, da_lambda_k2, da_subln_g,
                   mla_q_norm_g, mla_w_q_b, mla_kv_norm_g, mla_w_kv_b, w_branch_da, w_branch_mla, w_out,
                   ffn_norm_g, w_gate, w_up, w_down):
    w = w_in[l]
    cuts = [0, DA_COLS, 2 * DA_COLS, 3 * DA_COLS, 3 * DA_COLS + MLA_IN_COLS]
    pad = jnp.zeros((D_MODEL, LANES - MLA_ROPE_DIM), F32)
    head_dim = MLA_NOPE_DIM + MLA_ROPE_DIM
    wqb = mla_w_q_b[l].reshape(MLA_Q_RANK, MLA_HEADS, head_dim)
    wqb = jnp.pad(wqb, ((0, 0), (0, 0), (0, MLA_HEAD_BLOCK - head_dim)))
    return dict(
        attn_g=attn_norm_g[l],
        wq=w[:, cuts[0]:cuts[1]].astype(BF16),
        wk=w[:, cuts[1]:cuts[2]].astype(BF16),
        wv=w[:, cuts[2]:cuts[3]].astype(BF16),
        w_mla_in=jnp.concatenate([w[:, cuts[3]:cuts[4]], pad], axis=1).astype(BF16),
        w_gates=w[:, cuts[4]:].astype(BF16),
        lambdas=tuple(p[l].reshape(1, DA_QK_DIM) for p in (da_lambda_q1, da_lambda_k1, da_lambda_q2, da_lambda_k2)),
        subln_g=da_subln_g[l].reshape(1, DA_V_DIM),
        gq=mla_q_norm_g[l].reshape(1, MLA_Q_RANK),
        wqb=wqb.reshape(MLA_Q_RANK, MLA_HEADS * MLA_HEAD_BLOCK).astype(BF16),
        gkv=mla_kv_norm_g[l].reshape(1, MLA_KV_RANK),
        wkvb=mla_w_kv_b[l].astype(BF16),
        w_branch_da=w_branch_da[l].astype(BF16),
        w_branch_mla=w_branch_mla[l].astype(BF16),
        w_out=w_out[l].astype(BF16),
        ffn_g=ffn_norm_g[l],
        w_gate=w_gate[l].astype(BF16),
        w_up=w_up[l].astype(BF16),
        w_down=w_down[l].astype(BF16),
    )


def _layer(x, p, lambda_init, da_tables, mla_tables, final_norm_g, is_last):
    h = _rmsnorm_bf16(x, p["attn_g"])
    q_da = _proj(h, p["wq"], "rope_q", da_tables)
    k_da = _proj(h, p["wk"], "rope_k", da_tables)
    v_da = _proj(h, p["wv"], "plain")
    gates = _proj(h, p["w_gates"], "sigmoid")
    q_mla, k_mla, v_mla = _mla_pre(h, p["w_mla_in"], p["gq"], p["wqb"], p["gkv"], p["wkvb"], mla_tables)
    o_da = _flash(q_da, k_da, v_da, "da", extra=p["lambdas"] + (p["subln_g"],), lambda_init=lambda_init)
    o_mla = _flash(q_mla, k_mla, v_mla, "mla", tq=1024)
    x = _merge(o_da, o_mla, gates, p["w_branch_da"], p["w_branch_mla"], p["w_out"], x)
    return _ffn(x, p["ffn_g"], p["w_gate"], p["w_up"], p["w_down"], final_norm_g, final_norm=is_last)


def kernel(x_prompt, x_sample, attn_norm_g, w_in, da_lambda_q1, da_lambda_k1, da_lambda_q2, da_lambda_k2, da_subln_g, mla_q_norm_g, mla_w_q_b, mla_kv_norm_g, mla_w_kv_b, w_branch_da, w_branch_mla, w_out, ffn_norm_g, w_gate, w_up, w_down, final_norm_g):
    depth = w_in.shape[0]
    layers = [
        _prepare_layer(l, attn_norm_g, w_in, da_lambda_q1, da_lambda_k1, da_lambda_q2, da_lambda_k2, da_subln_g,
                       mla_q_norm_g, mla_w_q_b, mla_kv_norm_g, mla_w_kv_b, w_branch_da, w_branch_mla, w_out,
                       ffn_norm_g, w_gate, w_up, w_down)
        for l in range(depth)
    ]

    def trunk(x3):
        batch, seq, d = x3.shape
        da_tables = _rope_tables(seq, DA_ROT_DIM, DA_QK_DIM)
        mla_tables = _rope_tables(seq, MLA_ROPE_DIM, LANES)
        outs = []
        for bi in range(batch):
            x = x3[bi]
            for l, p in enumerate(layers):
                lambda_init = 0.8 - 0.6 * math.exp(-0.3 * l)
                x = _layer(x, p, lambda_init, da_tables, mla_tables, final_norm_g, is_last=(l == depth - 1))
            outs.append(x)
        return jnp.stack(outs) if batch > 1 else outs[0][None]

    return trunk(x_prompt), trunk(x_sample)
```

```python
import functools
import math

import jax
import jax.numpy as jnp
from jax import lax
from jax.experimental import pallas as pl
from jax.experimental.pallas import tpu as pltpu

D_MODEL = 2048
DA_HEADS = 8
DA_QK_DIM = 64
DA_V_DIM = 2 * DA_QK_DIM
DA_ROT_DIM = DA_QK_DIM // 4
MLA_HEADS = 8
MLA_Q_RANK = 512
MLA_KV_RANK = 512
MLA_NOPE_DIM = 128
MLA_ROPE_DIM = 64
MLA_V_DIM = 128
ROPE_THETA = 500000.0
NORM_EPS = 1e-6
FFN_HIDDEN = -(-8 * D_MODEL // (3 * 256)) * 256
DA_COLS = DA_HEADS * 2 * DA_QK_DIM
MLA_IN_COLS = MLA_Q_RANK + MLA_KV_RANK + MLA_ROPE_DIM
GATE_COLS = 2 * D_MODEL

LANES = 128
MLA_HEAD_BLOCK = 2 * LANES
VMEM_LIMIT_BYTES = 56 * 1024 * 1024
LOG2E = math.log2(math.e)
DA_Q_SCALE = DA_QK_DIM ** -0.5 * LOG2E
MLA_Q_SCALE = (MLA_NOPE_DIM + MLA_ROPE_DIM) ** -0.5 * LOG2E

BF16 = jnp.bfloat16
F32 = jnp.float32
_NT = (((1,), (1,)), ((), ()))


def _params(n_grid_dims):
    return pltpu.CompilerParams(
        dimension_semantics=("arbitrary",) * n_grid_dims,
        vmem_limit_bytes=VMEM_LIMIT_BYTES,
    )


def _rms(x, g):
    var = jnp.mean(x * x, axis=-1, keepdims=True)
    return x * lax.rsqrt(var + NORM_EPS) * g


def _rope_tables(seq, rot_dim, period):
    half = rot_dim // 2
    pos = jnp.arange(seq, dtype=F32)
    inv_freq = ROPE_THETA ** (-jnp.arange(0, rot_dim, 2, dtype=F32) / rot_dim)
    ang = pos[:, None] * inv_freq[None, :]
    cos, sin = jnp.cos(ang), jnp.sin(ang)
    rest = period - rot_dim
    zeros_half = jnp.zeros((seq, half), F32)
    zeros_rest = jnp.zeros((seq, rest), F32)
    a = jnp.concatenate([cos, cos, jnp.ones((seq, rest), F32)], axis=-1)
    b = jnp.concatenate([-sin, zeros_half, zeros_rest], axis=-1)
    c = jnp.concatenate([zeros_half, sin, zeros_rest], axis=-1)
    reps = LANES // period
    return tuple(jnp.tile(t, (1, reps)) for t in (a, b, c))


def _rope_tile(x, a, b, c, half):
    return x * a + pltpu.roll(x, LANES - half, 1) * b + pltpu.roll(x, half, 1) * c


def _rmsnorm_kernel(x_ref, g_ref, o_ref):
    o_ref[...] = _rms(x_ref[...], g_ref[...]).astype(o_ref.dtype)


def _rmsnorm_bf16(x, g, tm=512):
    s, d = x.shape
    return pl.pallas_call(
        _rmsnorm_kernel,
        grid=(s // tm,),
        in_specs=[pl.BlockSpec((tm, d), lambda i: (i, 0)), pl.BlockSpec((1, d), lambda i: (0, 0))],
        out_specs=pl.BlockSpec((tm, d), lambda i: (i, 0)),
        out_shape=jax.ShapeDtypeStruct((s, d), BF16),
        compiler_params=_params(1),
        name="rmsnorm",
    )(x, g.reshape(1, d))


def _proj_kernel(*refs, mode):
    if mode in ("rope_q", "rope_k"):
        h_ref, w_ref, a_ref, b_ref, c_ref, o_ref = refs
    else:
        h_ref, w_ref, o_ref = refs
    acc = jnp.dot(h_ref[...], w_ref[...], preferred_element_type=F32)
    if mode == "plain":
        o_ref[...] = acc.astype(o_ref.dtype)
        return
    if mode == "sigmoid":
        o_ref[...] = jax.nn.sigmoid(acc)
        return
    a, b, c = a_ref[...], b_ref[...], c_ref[...]
    tm, tn = acc.shape
    first_component = lax.broadcasted_iota(jnp.int32, (tm, LANES), 1) < DA_QK_DIM
    for t in range(tn // LANES):
        cols = slice(t * LANES, (t + 1) * LANES)
        y = _rope_tile(acc[:, cols], a, b, c, DA_ROT_DIM // 2)
        if mode == "rope_k":
            o_ref[:, cols] = y.astype(o_ref.dtype)
        else:
            y = y * DA_Q_SCALE
            o_ref[0, :, cols] = jnp.where(first_component, y, 0.0).astype(o_ref.dtype)
            o_ref[1, :, cols] = jnp.where(first_component, 0.0, y).astype(o_ref.dtype)


def _proj(h, w, mode, tables=(), tm=512, tn=512):
    s, k = h.shape
    n = w.shape[1]
    in_specs = [pl.BlockSpec((tm, k), lambda i, j: (i, 0)), pl.BlockSpec((k, tn), lambda i, j: (0, j))]
    in_specs += [pl.BlockSpec((tm, LANES), lambda i, j: (i, 0)) for _ in tables]
    if mode == "rope_q":
        out_shape = jax.ShapeDtypeStruct((2, s, n), BF16)
        out_spec = pl.BlockSpec((2, tm, tn), lambda i, j: (0, i, j))
    else:
        out_shape = jax.ShapeDtypeStruct((s, n), F32 if mode == "sigmoid" else BF16)
        out_spec = pl.BlockSpec((tm, tn), lambda i, j: (i, j))
    return pl.pallas_call(
        functools.partial(_proj_kernel, mode=mode),
        grid=(s // tm, n // tn),
        in_specs=in_specs,
        out_specs=out_spec,
        out_shape=out_shape,
        compiler_params=_params(2),
        name="proj_" + mode,
    )(h, w, *tables)


def _mla_pre_kernel(h_ref, win_ref, gq_ref, wqb_ref, gkv_ref, wkvb_ref, a_ref, b_ref, c_ref,
                    q_ref, k_ref, v_ref):
    a, b, c = a_ref[...], b_ref[...], c_ref[...]
    half = MLA_ROPE_DIM // 2
    lat = jnp.dot(h_ref[...], win_ref[...], preferred_element_type=F32)
    cq = _rms(lat[:, :MLA_Q_RANK], gq_ref[...]).astype(BF16)
    ckv = _rms(lat[:, MLA_Q_RANK:MLA_Q_RANK + MLA_KV_RANK], gkv_ref[...]).astype(BF16)
    k_rope = _rope_tile(lat[:, MLA_Q_RANK + MLA_KV_RANK:], a, b, c, half).astype(BF16)
    q = jnp.dot(cq, wqb_ref[...], preferred_element_type=F32)
    kv = jnp.dot(ckv, wkvb_ref[...], preferred_element_type=F32)
    for hd in range(MLA_HEADS):
        lo = hd * MLA_HEAD_BLOCK
        mid = lo + LANES
        hi = lo + MLA_HEAD_BLOCK
        q_ref[:, lo:mid] = (q[:, lo:mid] * MLA_Q_SCALE).astype(BF16)
        q_ref[:, mid:hi] = (_rope_tile(q[:, mid:hi], a, b, c, half) * MLA_Q_SCALE).astype(BF16)
        k_ref[:, lo:mid] = kv[:, lo:mid].astype(BF16)
        k_ref[:, mid:hi] = k_rope
        v_ref[:, hd * LANES:(hd + 1) * LANES] = kv[:, mid:hi].astype(BF16)


def _mla_pre(h, w_mla_in, gq, wqb, gkv, wkvb, tables, tm=256):
    s, d = h.shape
    qk_cols = MLA_HEADS * MLA_HEAD_BLOCK
    v_cols = MLA_HEADS * MLA_V_DIM
    full = lambda arr: pl.BlockSpec(arr.shape, lambda i: (0, 0))
    row = lambda n: pl.BlockSpec((tm, n), lambda i: (i, 0))
    return pl.pallas_call(
        _mla_pre_kernel,
        grid=(s // tm,),
        in_specs=[row(d), full(w_mla_in), full(gq), full(wqb), full(gkv), full(wkvb),
                  row(LANES), row(LANES), row(LANES)],
        out_specs=[row(qk_cols), row(qk_cols), row(v_cols)],
        out_shape=[jax.ShapeDtypeStruct((s, qk_cols), BF16),
                   jax.ShapeDtypeStruct((s, qk_cols), BF16),
                   jax.ShapeDtypeStruct((s, v_cols), BF16)],
        compiler_params=_params(1),
        name="mla_pre",
    )(h, w_mla_in, gq, wqb, gkv, wkvb, *tables)


def _flash_kernel(*refs, mode, tk, lambda_init):
    if mode == "da":
        (q_ref, k_ref, v_ref, lq1_ref, lk1_ref, lq2_ref, lk2_ref, g_ref, o_ref,
         s_sc, p_sc, alpha_sc, m_sc, l_sc, acc_sc) = refs
    else:
        q_ref, k_ref, v_ref, o_ref, s_sc, p_sc, alpha_sc, m_sc, l_sc, acc_sc = refs
    rows = m_sc.shape[0]
    n_kv = k_ref.shape[0] // tk
    q = q_ref[...].reshape(rows, q_ref.shape[-1])
    m_sc[...] = jnp.full(m_sc.shape, -jnp.inf, F32)
    l_sc[...] = jnp.zeros(l_sc.shape, F32)
    acc_sc[...] = jnp.zeros(acc_sc.shape, F32)

    def chunk(j):
        return pl.ds(pl.multiple_of(j * tk, tk), tk)

    def scores(j, slot):
        s_sc[slot] = lax.dot_general(q, k_ref[chunk(j), :], _NT, preferred_element_type=F32)

    def softmax(slot):
        s = s_sc[slot]
        m_prev = m_sc[...]
        m_new = jnp.maximum(m_prev, jnp.max(s, axis=-1, keepdims=True))
        alpha = jnp.exp2(m_prev - m_new)
        l_part = alpha * l_sc[...]
        for t in range(tk // LANES):
            cols = slice(t * LANES, (t + 1) * LANES)
            p = jnp.exp2(s[:, cols] - m_new)
            l_part = l_part + p
            p_sc[slot, :, cols] = p.astype(BF16)
        l_sc[...] = l_part
        alpha_sc[slot] = alpha
        m_sc[...] = m_new

    def accumulate(j, slot):
        pv = jnp.dot(p_sc[slot], v_ref[chunk(j), :], preferred_element_type=F32)
        acc_sc[...] = alpha_sc[slot] * acc_sc[...] + pv

    scores(0, 0)
    scores(1, 1)
    softmax(0)

    def kv_pair(i, carry):
        t = 2 * i + 2
        scores(t, 0)
        softmax(1)
        accumulate(t - 2, 0)
        scores(t + 1, 1)
        softmax(0)
        accumulate(t - 1, 1)
        return carry

    lax.fori_loop(0, n_kv // 2 - 1, kv_pair, 0)
    softmax(1)
    accumulate(n_kv - 2, 0)
    accumulate(n_kv - 1, 1)

    o = acc_sc[...] / jnp.sum(l_sc[...], axis=-1, keepdims=True)
    if mode == "da":
        tq = rows // 2
        lam = (jnp.exp(jnp.sum(lq1_ref[...] * lk1_ref[...])) - jnp.exp(jnp.sum(lq2_ref[...] * lk2_ref[...]))
               + lambda_init)
        diff = o[:tq] - lam * o[tq:]
        o = _rms(diff, g_ref[...]) * (1.0 - lambda_init)
    o_ref[...] = o.astype(o_ref.dtype)


def _flash(q, k, v, mode, extra=(), lambda_init=0.0, tq=512, tk=512):
    s = k.shape[0]
    heads = v.shape[1] // LANES
    dk = k.shape[1] // heads
    if mode == "da":
        rows = 2 * tq
        q_spec = pl.BlockSpec((2, tq, dk), lambda h, i: (0, i, h))
    else:
        rows = tq
        q_spec = pl.BlockSpec((tq, dk), lambda h, i: (i, h))
    in_specs = [q_spec,
                pl.BlockSpec((s, dk), lambda h, i: (0, h)),
                pl.BlockSpec((s, LANES), lambda h, i: (0, h))]
    in_specs += [pl.BlockSpec(e.shape, lambda h, i: (0, 0)) for e in extra]
    return pl.pallas_call(
        functools.partial(_flash_kernel, mode=mode, tk=tk, lambda_init=lambda_init),
        grid=(heads, s // tq),
        in_specs=in_specs,
        out_specs=pl.BlockSpec((tq, LANES), lambda h, i: (i, h)),
        out_shape=jax.ShapeDtypeStruct((s, heads * LANES), BF16),
        scratch_shapes=[pltpu.VMEM((2, rows, tk), F32), pltpu.VMEM((2, rows, tk), BF16),
                        pltpu.VMEM((2, rows, LANES), F32), pltpu.VMEM((rows, LANES), F32),
                        pltpu.VMEM((rows, LANES), F32), pltpu.VMEM((rows, LANES), F32)],
        compiler_params=_params(2),
        name="flash_" + mode,
    )(q, k, v, *extra)


def _merge_kernel(oda_ref, omla_ref, gda_ref, gmla_ref, wda_ref, wmla_ref, wo_ref, x_ref, o_ref, acc_sc):
    c = pl.program_id(1)

    @pl.when(c == 0)
    def _():
        acc_sc[...] = jnp.zeros(acc_sc.shape, F32)

    b_da = jnp.dot(oda_ref[...], wda_ref[...], preferred_element_type=F32)
    b_mla = jnp.dot(omla_ref[...], wmla_ref[...], preferred_element_type=F32)
    merged = gda_ref[...] * b_da + gmla_ref[...] * b_mla
    acc_sc[...] += jnp.dot(merged.astype(BF16), wo_ref[...], preferred_element_type=F32)

    @pl.when(c == pl.num_programs(1) - 1)
    def _():
        o_ref[...] = x_ref[...] + acc_sc[...]


def _merge(o_da, o_mla, gates, w_da, w_mla, w_out, x, tm=512, tc=512):
    s, d = x.shape
    n_c = d // tc
    return pl.pallas_call(
        _merge_kernel,
        grid=(s // tm, n_c),
        in_specs=[pl.BlockSpec((tm, o_da.shape[1]), lambda i, c: (i, 0)),
                  pl.BlockSpec((tm, o_mla.shape[1]), lambda i, c: (i, 0)),
                  pl.BlockSpec((tm, tc), lambda i, c: (i, c)),
                  pl.BlockSpec((tm, tc), lambda i, c: (i, c + n_c)),
                  pl.BlockSpec((w_da.shape[0], tc), lambda i, c: (0, c)),
                  pl.BlockSpec((w_mla.shape[0], tc), lambda i, c: (0, c)),
                  pl.BlockSpec((tc, d), lambda i, c: (c, 0)),
                  pl.BlockSpec((tm, d), lambda i, c: (i, 0))],
        out_specs=pl.BlockSpec((tm, d), lambda i, c: (i, 0)),
        out_shape=jax.ShapeDtypeStruct((s, d), F32),
        scratch_shapes=[pltpu.VMEM((tm, d), F32)],
        compiler_params=_params(2),
        name="merge",
    )(o_da, o_mla, gates, gates, w_da, w_mla, w_out, x)


def _ffn_kernel(x_ref, g_ref, wg_ref, wu_ref, wd_ref, gf_ref, o_ref, h_sc, acc_sc, *, final_norm):
    f = pl.program_id(1)

    @pl.when(f == 0)
    def _():
        h_sc[...] = _rms(x_ref[...], g_ref[...]).astype(h_sc.dtype)
        acc_sc[...] = jnp.zeros(acc_sc.shape, F32)

    h = h_sc[...]
    gate = jnp.dot(h, wg_ref[...], preferred_element_type=F32)
    up = jnp.dot(h, wu_ref[...], preferred_element_type=F32)
    act = (jax.nn.silu(gate) * up).astype(BF16)
    acc_sc[...] += jnp.dot(act, wd_ref[...], preferred_element_type=F32)

    @pl.when(f == pl.num_programs(1) - 1)
    def _():
        y = x_ref[...] + acc_sc[...]
        o_ref[...] = _rms(y, gf_ref[...]) if final_norm else y


def _ffn(x, g, w_gate, w_up, w_down, g_final, final_norm, tm=512, tf=512):
    s, d = x.shape
    hidden = w_gate.shape[1]
    return pl.pallas_call(
        functools.partial(_ffn_kernel, final_norm=final_norm),
        grid=(s // tm, hidden // tf),
        in_specs=[pl.BlockSpec((tm, d), lambda i, f: (i, 0)),
                  pl.BlockSpec((1, d), lambda i, f: (0, 0)),
                  pl.BlockSpec((d, tf), lambda i, f: (0, f)),
                  pl.BlockSpec((d, tf), lambda i, f: (0, f)),
                  pl.BlockSpec((tf, d), lambda i, f: (f, 0)),
                  pl.BlockSpec((1, d), lambda i, f: (0, 0))],
        out_specs=pl.BlockSpec((tm, d), lambda i, f: (i, 0)),
        out_shape=jax.ShapeDtypeStruct((s, d), F32),
        scratch_shapes=[pltpu.VMEM((tm, d), BF16), pltpu.VMEM((tm, d), F32)],
        compiler_params=_params(2),
        name="ffn",
    )(x, g.reshape(1, d), w_gate, w_up, w_down, g_final.reshape(1, d))


def _prepare_layer(l, attn_norm_g, w_in, da_lambda_q1, da_lambda_k1, da_lambda_q2, da_lambda_k2, da_subln_g,
                   mla_q_norm_g, mla_w_q_b, mla_kv_norm_g, mla_w_kv_b, w_branch_da, w_branch_mla, w_out,
                   ffn_norm_g, w_gate, w_up, w_down):
    w = w_in[l]
    cuts = [0, DA_COLS, 2 * DA_COLS, 3 * DA_COLS, 3 * DA_COLS + MLA_IN_COLS]
    pad = jnp.zeros((D_MODEL, LANES - MLA_ROPE_DIM), F32)
    head_dim = MLA_NOPE_DIM + MLA_ROPE_DIM
    wqb = mla_w_q_b[l].reshape(MLA_Q_RANK, MLA_HEADS, head_dim)
    wqb = jnp.pad(wqb, ((0, 0), (0, 0), (0, MLA_HEAD_BLOCK - head_dim)))
    return dict(
        attn_g=attn_norm_g[l],
        wq=w[:, cuts[0]:cuts[1]].astype(BF16),
        wk=w[:, cuts[1]:cuts[2]].astype(BF16),
        wv=w[:, cuts[2]:cuts[3]].astype(BF16),
        w_mla_in=jnp.concatenate([w[:, cuts[3]:cuts[4]], pad], axis=1).astype(BF16),
        w_gates=w[:, cuts[4]:].astype(BF16),
        lambdas=tuple(p[l].reshape(1, DA_QK_DIM) for p in (da_lambda_q1, da_lambda_k1, da_lambda_q2, da_lambda_k2)),
        subln_g=da_subln_g[l].reshape(1, DA_V_DIM),
        gq=mla_q_norm_g[l].reshape(1, MLA_Q_RANK),
        wqb=wqb.reshape(MLA_Q_RANK, MLA_HEADS * MLA_HEAD_BLOCK).astype(BF16),
        gkv=mla_kv_norm_g[l].reshape(1, MLA_KV_RANK),
        wkvb=mla_w_kv_b[l].astype(BF16),
        w_branch_da=w_branch_da[l].astype(BF16),
        w_branch_mla=w_branch_mla[l].astype(BF16),
        w_out=w_out[l].astype(BF16),
        ffn_g=ffn_norm_g[l],
        w_gate=w_gate[l].astype(BF16),
        w_up=w_up[l].astype(BF16),
        w_down=w_down[l].astype(BF16),
    )


def _layer(x, p, lambda_init, da_tables, mla_tables, final_norm_g, is_last):
    h = _rmsnorm_bf16(x, p["attn_g"])
    q_da = _proj(h, p["wq"], "rope_q", da_tables)
    k_da = _proj(h, p["wk"], "rope_k", da_tables)
    v_da = _proj(h, p["wv"], "plain")
    gates = _proj(h, p["w_gates"], "sigmoid")
    q_mla, k_mla, v_mla = _mla_pre(h, p["w_mla_in"], p["gq"], p["wqb"], p["gkv"], p["wkvb"], mla_tables)
    o_da = _flash(q_da, k_da, v_da, "da", extra=p["lambdas"] + (p["subln_g"],), lambda_init=lambda_init)
    o_mla = _flash(q_mla, k_mla, v_mla, "mla", tq=1024)
    x = _merge(o_da, o_mla, gates, p["w_branch_da"], p["w_branch_mla"], p["w_out"], x)
    return _ffn(x, p["ffn_g"], p["w_gate"], p["w_up"], p["w_down"], final_norm_g, final_norm=is_last)


def kernel(x_prompt, x_sample, attn_norm_g, w_in, da_lambda_q1, da_lambda_k1, da_lambda_q2, da_lambda_k2, da_subln_g, mla_q_norm_g, mla_w_q_b, mla_kv_norm_g, mla_w_kv_b, w_branch_da, w_branch_mla, w_out, ffn_norm_g, w_gate, w_up, w_down, final_norm_g):
    depth = w_in.shape[0]
    layers = [
        _prepare_layer(l, attn_norm_g, w_in, da_lambda_q1, da_lambda_k1, da_lambda_q2, da_lambda_k2, da_subln_g,
                       mla_q_norm_g, mla_w_q_b, mla_kv_norm_g, mla_w_kv_b, w_branch_da, w_branch_mla, w_out,
                       ffn_norm_g, w_gate, w_up, w_down)
        for l in range(depth)
    ]

    def trunk(x3):
        batch, seq, d = x3.shape
        da_tables = _rope_tables(seq, DA_ROT_DIM, DA_QK_DIM)
        mla_tables = _rope_tables(seq, MLA_ROPE_DIM, LANES)
        outs = []
        for bi in range(batch):
            x = x3[bi]
            for l, p in enumerate(layers):
                lambda_init = 0.8 - 0.6 * math.exp(-0.3 * l)
                x = _layer(x, p, lambda_init, da_tables, mla_tables, final_norm_g, is_last=(l == depth - 1))
            outs.append(x)
        return jnp.stack(outs) if batch > 1 else outs[0][None]

    return trunk(x_prompt), trunk(x_sample)
```

```python
import functools
import math

import jax
import jax.numpy as jnp
from jax import lax
from jax.experimental import pallas as pl
from jax.experimental.pallas import tpu as pltpu

D_MODEL = 2048
DA_HEADS = 8
DA_QK_DIM = 64
DA_V_DIM = 2 * DA_QK_DIM
DA_ROT_DIM = DA_QK_DIM // 4
MLA_HEADS = 8
MLA_Q_RANK = 512
MLA_KV_RANK = 512
MLA_NOPE_DIM = 128
MLA_ROPE_DIM = 64
MLA_V_DIM = 128
ROPE_THETA = 500000.0
NORM_EPS = 1e-6
FFN_HIDDEN = -(-8 * D_MODEL // (3 * 256)) * 256
DA_COLS = DA_HEADS * 2 * DA_QK_DIM
MLA_IN_COLS = MLA_Q_RANK + MLA_KV_RANK + MLA_ROPE_DIM
GATE_COLS = 2 * D_MODEL

LANES = 128
MLA_HEAD_BLOCK = 2 * LANES
VMEM_LIMIT_BYTES = 56 * 1024 * 1024
LOG2E = math.log2(math.e)
DA_Q_SCALE = DA_QK_DIM ** -0.5 * LOG2E
MLA_Q_SCALE = (MLA_NOPE_DIM + MLA_ROPE_DIM) ** -0.5 * LOG2E

BF16 = jnp.bfloat16
F32 = jnp.float32
_NT = (((1,), (1,)), ((), ()))


def _params(n_grid_dims):
    return pltpu.CompilerParams(
        dimension_semantics=("arbitrary",) * n_grid_dims,
        vmem_limit_bytes=VMEM_LIMIT_BYTES,
    )


def _rms(x, g):
    var = jnp.mean(x * x, axis=-1, keepdims=True)
    return x * lax.rsqrt(var + NORM_EPS) * g


def _rope_tables(seq, rot_dim, period):
    half = rot_dim // 2
    pos = jnp.arange(seq, dtype=F32)
    inv_freq = ROPE_THETA ** (-jnp.arange(0, rot_dim, 2, dtype=F32) / rot_dim)
    ang = pos[:, None] * inv_freq[None, :]
    cos, sin = jnp.cos(ang), jnp.sin(ang)
    rest = period - rot_dim
    zeros_half = jnp.zeros((seq, half), F32)
    zeros_rest = jnp.zeros((seq, rest), F32)
    a = jnp.concatenate([cos, cos, jnp.ones((seq, rest), F32)], axis=-1)
    b = jnp.concatenate([-sin, zeros_half, zeros_rest], axis=-1)
    c = jnp.concatenate([zeros_half, sin, zeros_rest], axis=-1)
    reps = LANES // period
    return tuple(jnp.tile(t, (1, reps)) for t in (a, b, c))


def _rope_tile(x, a, b, c, half):
    return x * a + pltpu.roll(x, LANES - half, 1) * b + pltpu.roll(x, half, 1) * c


def _rmsnorm_kernel(x_ref, g_ref, o_ref):
    o_ref[...] = _rms(x_ref[...], g_ref[...]).astype(o_ref.dtype)


def _rmsnorm_bf16(x, g, tm=512):
    s, d = x.shape
    return pl.pallas_call(
        _rmsnorm_kernel,
        grid=(s // tm,),
        in_specs=[pl.BlockSpec((tm, d), lambda i: (i, 0)), pl.BlockSpec((1, d), lambda i: (0, 0))],
        out_specs=pl.BlockSpec((tm, d), lambda i: (i, 0)),
        out_shape=jax.ShapeDtypeStruct((s, d), BF16),
        compiler_params=_params(1),
        name="rmsnorm",
    )(x, g.reshape(1, d))


def _proj_kernel(*refs, mode):
    if mode in ("rope_q", "rope_k"):
        h_ref, w_ref, a_ref, b_ref, c_ref, o_ref = refs
    else:
        h_ref, w_ref, o_ref = refs
    acc = jnp.dot(h_ref[...], w_ref[...], preferred_element_type=F32)
    if mode == "plain":
        o_ref[...] = acc.astype(o_ref.dtype)
        return
    if mode == "sigmoid":
        o_ref[...] = jax.nn.sigmoid(acc)
        return
    a, b, c = a_ref[...], b_ref[...], c_ref[...]
    tm, tn = acc.shape
    first_component = lax.broadcasted_iota(jnp.int32, (tm, LANES), 1) < DA_QK_DIM
    for t in range(tn // LANES):
        cols = slice(t * LANES, (t + 1) * LANES)
        y = _rope_tile(acc[:, cols], a, b, c, DA_ROT_DIM // 2)
        if mode == "rope_k":
            o_ref[:, cols] = y.astype(o_ref.dtype)
        else:
            y = y * DA_Q_SCALE
            o_ref[0, :, cols] = jnp.where(first_component, y, 0.0).astype(o_ref.dtype)
            o_ref[1, :, cols] = jnp.where(first_component, 0.0, y).astype(o_ref.dtype)


def _proj(h, w, mode, tables=(), tm=1024, tn=1024):
    s, k = h.shape
    n = w.shape[1]
    in_specs = [pl.BlockSpec((tm, k), lambda i, j: (i, 0)), pl.BlockSpec((k, tn), lambda i, j: (0, j))]
    in_specs += [pl.BlockSpec((tm, LANES), lambda i, j: (i, 0)) for _ in tables]
    if mode == "rope_q":
        out_shape = jax.ShapeDtypeStruct((2, s, n), BF16)
        out_spec = pl.BlockSpec((2, tm, tn), lambda i, j: (0, i, j))
    else:
        out_shape = jax.ShapeDtypeStruct((s, n), F32 if mode == "sigmoid" else BF16)
        out_spec = pl.BlockSpec((tm, tn), lambda i, j: (i, j))
    return pl.pallas_call(
        functools.partial(_proj_kernel, mode=mode),
        grid=(s // tm, n // tn),
        in_specs=in_specs,
        out_specs=out_spec,
        out_shape=out_shape,
        compiler_params=_params(2),
        name="proj_" + mode,
    )(h, w, *tables)


def _mla_pre_kernel(h_ref, win_ref, gq_ref, wqb_ref, gkv_ref, wkvb_ref, a_ref, b_ref, c_ref,
                    q_ref, k_ref, v_ref):
    a, b, c = a_ref[...], b_ref[...], c_ref[...]
    half = MLA_ROPE_DIM // 2
    lat = jnp.dot(h_ref[...], win_ref[...], preferred_element_type=F32)
    cq = _rms(lat[:, :MLA_Q_RANK], gq_ref[...]).astype(BF16)
    ckv = _rms(lat[:, MLA_Q_RANK:MLA_Q_RANK + MLA_KV_RANK], gkv_ref[...]).astype(BF16)
    k_rope = _rope_tile(lat[:, MLA_Q_RANK + MLA_KV_RANK:], a, b, c, half).astype(BF16)
    q = jnp.dot(cq, wqb_ref[...], preferred_element_type=F32)
    kv = jnp.dot(ckv, wkvb_ref[...], preferred_element_type=F32)
    for hd in range(MLA_HEADS):
        lo = hd * MLA_HEAD_BLOCK
        mid = lo + LANES
        hi = lo + MLA_HEAD_BLOCK
        q_ref[:, lo:mid] = (q[:, lo:mid] * MLA_Q_SCALE).astype(BF16)
        q_ref[:, mid:hi] = (_rope_tile(q[:, mid:hi], a, b, c, half) * MLA_Q_SCALE).astype(BF16)
        k_ref[:, lo:mid] = kv[:, lo:mid].astype(BF16)
        k_ref[:, mid:hi] = k_rope
        v_ref[:, hd * LANES:(hd + 1) * LANES] = kv[:, mid:hi].astype(BF16)


def _mla_pre(h, w_mla_in, gq, wqb, gkv, wkvb, tables, tm=256):
    s, d = h.shape
    qk_cols = MLA_HEADS * MLA_HEAD_BLOCK
    v_cols = MLA_HEADS * MLA_V_DIM
    full = lambda arr: pl.BlockSpec(arr.shape, lambda i: (0, 0))
    row = lambda n: pl.BlockSpec((tm, n), lambda i: (i, 0))
    return pl.pallas_call(
        _mla_pre_kernel,
        grid=(s // tm,),
        in_specs=[row(d), full(w_mla_in), full(gq), full(wqb), full(gkv), full(wkvb),
                  row(LANES), row(LANES), row(LANES)],
        out_specs=[row(qk_cols), row(qk_cols), row(v_cols)],
        out_shape=[jax.ShapeDtypeStruct((s, qk_cols), BF16),
                   jax.ShapeDtypeStruct((s, qk_cols), BF16),
                   jax.ShapeDtypeStruct((s, v_cols), BF16)],
        compiler_params=_params(1),
        name="mla_pre",
    )(h, w_mla_in, gq, wqb, gkv, wkvb, *tables)


def _flash_kernel(*refs, mode, tk, lambda_init):
    if mode == "da":
        (q_ref, k_ref, v_ref, lq1_ref, lk1_ref, lq2_ref, lk2_ref, g_ref, o_ref,
         s_sc, p_sc, alpha_sc, m_sc, acc_sc) = refs
    else:
        q_ref, k_ref, v_ref, o_ref, s_sc, p_sc, alpha_sc, m_sc, acc_sc = refs
    rows = m_sc.shape[0]
    n_kv = k_ref.shape[0] // tk
    q = q_ref[...].reshape(rows, q_ref.shape[-1])
    m_sc[...] = jnp.full(m_sc.shape, -jnp.inf, F32)
    acc_sc[...] = jnp.zeros(acc_sc.shape, F32)

    def chunk(j):
        return pl.ds(pl.multiple_of(j * tk, tk), tk)

    def scores(j, slot):
        s_sc[slot] = lax.dot_general(q, k_ref[chunk(j), :], _NT, preferred_element_type=F32)

    def softmax(slot):
        s = s_sc[slot]
        m_prev = m_sc[...]
        m_new = jnp.maximum(m_prev, jnp.max(s, axis=-1, keepdims=True))
        alpha_sc[slot] = jnp.exp2(m_prev - m_new)
        for t in range(tk // LANES):
            cols = slice(t * LANES, (t + 1) * LANES)
            p_sc[slot, :, cols] = jnp.exp2((s[:, cols] - m_new).astype(BF16))
        m_sc[...] = m_new

    ones = jnp.ones((tk, LANES), BF16)

    def accumulate(j, slot):
        v_and_ones = jnp.concatenate([v_ref[chunk(j), :], ones], axis=1)
        pv = jnp.dot(p_sc[slot], v_and_ones, preferred_element_type=F32)
        alpha = alpha_sc[slot]
        acc_sc[:, :LANES] = alpha * acc_sc[:, :LANES] + pv[:, :LANES]
        acc_sc[:, LANES:] = alpha * acc_sc[:, LANES:] + pv[:, LANES:]

    scores(0, 0)
    scores(1, 1)
    softmax(0)

    def kv_pair(i, carry):
        t = 2 * i + 2
        scores(t, 0)
        softmax(1)
        accumulate(t - 2, 0)
        scores(t + 1, 1)
        softmax(0)
        accumulate(t - 1, 1)
        return carry

    lax.fori_loop(0, n_kv // 2 - 1, kv_pair, 0)
    softmax(1)
    accumulate(n_kv - 2, 0)
    accumulate(n_kv - 1, 1)

    o = acc_sc[:, :LANES] / acc_sc[:, LANES:]
    if mode == "da":
        tq = rows // 2
        lam = (jnp.exp(jnp.sum(lq1_ref[...] * lk1_ref[...])) - jnp.exp(jnp.sum(lq2_ref[...] * lk2_ref[...]))
               + lambda_init)
        diff = o[:tq] - lam * o[tq:]
        o = _rms(diff, g_ref[...]) * (1.0 - lambda_init)
    o_ref[...] = o.astype(o_ref.dtype)


def _flash(q, k, v, mode, extra=(), lambda_init=0.0, tq=512, tk=512):
    s = k.shape[0]
    heads = v.shape[1] // LANES
    dk = k.shape[1] // heads
    if mode == "da":
        rows = 2 * tq
        q_spec = pl.BlockSpec((2, tq, dk), lambda h, i: (0, i, h))
    else:
        rows = tq
        q_spec = pl.BlockSpec((tq, dk), lambda h, i: (i, h))
    in_specs = [q_spec,
                pl.BlockSpec((s, dk), lambda h, i: (0, h)),
                pl.BlockSpec((s, LANES), lambda h, i: (0, h))]
    in_specs += [pl.BlockSpec(e.shape, lambda h, i: (0, 0)) for e in extra]
    return pl.pallas_call(
        functools.partial(_flash_kernel, mode=mode, tk=tk, lambda_init=lambda_init),
        grid=(heads, s // tq),
        in_specs=in_specs,
        out_specs=pl.BlockSpec((tq, LANES), lambda h, i: (i, h)),
        out_shape=jax.ShapeDtypeStruct((s, heads * LANES), BF16),
        scratch_shapes=[pltpu.VMEM((2, rows, tk), F32), pltpu.VMEM((2, rows, tk), BF16),
                        pltpu.VMEM((2, rows, LANES), F32), pltpu.VMEM((rows, LANES), F32),
                        pltpu.VMEM((rows, 2 * LANES), F32)],
        compiler_params=_params(2),
        name="flash_" + mode,
    )(q, k, v, *extra)


def _merge_kernel(oda_ref, omla_ref, gda_ref, gmla_ref, wda_ref, wmla_ref, wo_ref, x_ref, o_ref, acc_sc):
    c = pl.program_id(1)

    @pl.when(c == 0)
    def _():
        acc_sc[...] = jnp.zeros(acc_sc.shape, F32)

    b_da = jnp.dot(oda_ref[...], wda_ref[...], preferred_element_type=F32)
    b_mla = jnp.dot(omla_ref[...], wmla_ref[...], preferred_element_type=F32)
    merged = gda_ref[...] * b_da + gmla_ref[...] * b_mla
    acc_sc[...] += jnp.dot(merged.astype(BF16), wo_ref[...], preferred_element_type=F32)

    @pl.when(c == pl.num_programs(1) - 1)
    def _():
        o_ref[...] = x_ref[...] + acc_sc[...]


def _merge(o_da, o_mla, gates, w_da, w_mla, w_out, x, tm=512, tc=1024):
    s, d = x.shape
    n_c = d // tc
    return pl.pallas_call(
        _merge_kernel,
        grid=(s // tm, n_c),
        in_specs=[pl.BlockSpec((tm, o_da.shape[1]), lambda i, c: (i, 0)),
                  pl.BlockSpec((tm, o_mla.shape[1]), lambda i, c: (i, 0)),
                  pl.BlockSpec((tm, tc), lambda i, c: (i, c)),
                  pl.BlockSpec((tm, tc), lambda i, c: (i, c + n_c)),
                  pl.BlockSpec((w_da.shape[0], tc), lambda i, c: (0, c)),
                  pl.BlockSpec((w_mla.shape[0], tc), lambda i, c: (0, c)),
                  pl.BlockSpec((tc, d), lambda i, c: (c, 0)),
                  pl.BlockSpec((tm, d), lambda i, c: (i, 0))],
        out_specs=pl.BlockSpec((tm, d), lambda i, c: (i, 0)),
        out_shape=jax.ShapeDtypeStruct((s, d), F32),
        scratch_shapes=[pltpu.VMEM((tm, d), F32)],
        compiler_params=_params(2),
        name="merge",
    )(o_da, o_mla, gates, gates, w_da, w_mla, w_out, x)


def _ffn_kernel(x_ref, g_ref, wg_ref, wu_ref, wd_ref, gf_ref, o_ref, h_sc, acc_sc, *, final_norm):
    f = pl.program_id(1)

    @pl.when(f == 0)
    def _():
        h_sc[...] = _rms(x_ref[...], g_ref[...]).astype(h_sc.dtype)
        acc_sc[...] = jnp.zeros(acc_sc.shape, F32)

    h = h_sc[...]
    gate = jnp.dot(h, wg_ref[...], preferred_element_type=F32)
    up = jnp.dot(h, wu_ref[...], preferred_element_type=F32)
    act = (jax.nn.silu(gate) * up).astype(BF16)
    acc_sc[...] += jnp.dot(act, wd_ref[...], preferred_element_type=F32)

    @pl.when(f == pl.num_programs(1) - 1)
    def _():
        y = x_ref[...] + acc_sc[...]
        o_ref[...] = _rms(y, gf_ref[...]) if final_norm else y


def _ffn(x, g, w_gate, w_up, w_down, g_final, final_norm, tm=512, tf=512):
    s, d = x.shape
    hidden = w_gate.shape[1]
    return pl.pallas_call(
        functools.partial(_ffn_kernel, final_norm=final_norm),
        grid=(s // tm, hidden // tf),
        in_specs=[pl.BlockSpec((tm, d), lambda i, f: (i, 0)),
                  pl.BlockSpec((1, d), lambda i, f: (0, 0)),
                  pl.BlockSpec((d, tf), lambda i, f: (0, f)),
                  pl.BlockSpec((d, tf), lambda i, f: (0, f)),
                  pl.BlockSpec((tf, d), lambda i, f: (f, 0)),
                  pl.BlockSpec((1, d), lambda i, f: (0, 0))],
        out_specs=pl.BlockSpec((tm, d), lambda i, f: (i, 0)),
        out_shape=jax.ShapeDtypeStruct((s, d), F32),
        scratch_shapes=[pltpu.VMEM((tm, d), BF16), pltpu.VMEM((tm, d), F32)],
        compiler_params=_params(2),
        name="ffn",
    )(x, g.reshape(1, d), w_gate, w_up, w_down, g_final.reshape(1, d))


def _prepare_layer(l, attn_norm_g, w_in, da_lambda_q1, da_lambda_k1, da_lambda_q2, da_lambda_k2, da_subln_g,
                   mla_q_norm_g, mla_w_q_b, mla_kv_norm_g, mla_w_kv_b, w_branch_da, w_branch_mla, w_out,
                   ffn_norm_g, w_gate, w_up, w_down):
    w = w_in[l]
    cuts = [0, DA_COLS, 2 * DA_COLS, 3 * DA_COLS, 3 * DA_COLS + MLA_IN_COLS]
    pad = jnp.zeros((D_MODEL, LANES - MLA_ROPE_DIM), F32)
    head_dim = MLA_NOPE_DIM + MLA_ROPE_DIM
    wqb = mla_w_q_b[l].reshape(MLA_Q_RANK, MLA_HEADS, head_dim)
    wqb = jnp.pad(wqb, ((0, 0), (0, 0), (0, MLA_HEAD_BLOCK - head_dim)))
    return dict(
        attn_g=attn_norm_g[l],
        wq=w[:, cuts[0]:cuts[1]].astype(BF16),
        wk=w[:, cuts[1]:cuts[2]].astype(BF16),
        wv=w[:, cuts[2]:cuts[3]].astype(BF16),
        w_mla_in=jnp.concatenate([w[:, cuts[3]:cuts[4]], pad], axis=1).astype(BF16),
        w_gates=w[:, cuts[4]:].astype(BF16),
        lambdas=tuple(p[l].reshape(1, DA_QK_DIM) for p in (da_lambda_q1, da_lambda_k1, da_lambda_q2, da_lambda_k2)),
        subln_g=da_subln_g[l].reshape(1, DA_V_DIM),
        gq=mla_q_norm_g[l].reshape(1, MLA_Q_RANK),
        wqb=wqb.reshape(MLA_Q_RANK, MLA_HEADS * MLA_HEAD_BLOCK).astype(BF16),
        gkv=mla_kv_norm_g[l].reshape(1, MLA_KV_RANK),
        wkvb=mla_w_kv_b[l].astype(BF16),
        w_branch_da=w_branch_da[l].astype(BF16),
        w_branch_mla=w_branch_mla[l].astype(BF16),
        w_out=w_out[l].astype(BF16),
        ffn_g=ffn_norm_g[l],
        w_gate=w_gate[l].astype(BF16),
        w_up=w_up[l].astype(BF16),
        w_down=w_down[l].astype(BF16),
    )


def _layer(x, p, lambda_init, da_tables, mla_tables, final_norm_g, is_last):
    h = _rmsnorm_bf16(x, p["attn_g"])
    q_da = _proj(h, p["wq"], "rope_q", da_tables)
    k_da = _proj(h, p["wk"], "rope_k", da_tables)
    v_da = _proj(h, p["wv"], "plain")
    gates = _proj(h, p["w_gates"], "sigmoid")
    q_mla, k_mla, v_mla = _mla_pre(h, p["w_mla_in"], p["gq"], p["wqb"], p["gkv"], p["wkvb"], mla_tables)
    o_da = _flash(q_da, k_da, v_da, "da", extra=p["lambdas"] + (p["subln_g"],), lambda_init=lambda_init)
    o_mla = _flash(q_mla, k_mla, v_mla, "mla", tq=1024)
    x = _merge(o_da, o_mla, gates, p["w_branch_da"], p["w_branch_mla"], p["w_out"], x)
    return _ffn(x, p["ffn_g"], p["w_gate"], p["w_up"], p["w_down"], final_norm_g, final_norm=is_last)


def kernel(x_prompt, x_sample, attn_norm_g, w_in, da_lambda_q1, da_lambda_k1, da_lambda_q2, da_lambda_k2, da_subln_g, mla_q_norm_g, mla_w_q_b, mla_kv_norm_g, mla_w_kv_b, w_branch_da, w_branch_mla, w_out, ffn_norm_g, w_gate, w_up, w_down, final_norm_g):
    depth = w_in.shape[0]
    layers = [
        _prepare_layer(l, attn_norm_g, w_in, da_lambda_q1, da_lambda_k1, da_lambda_q2, da_lambda_k2, da_subln_g,
                       mla_q_norm_g, mla_w_q_b, mla_kv_norm_g, mla_w_kv_b, w_branch_da, w_branch_mla, w_out,
                       ffn_norm_g, w_gate, w_up, w_down)
        for l in range(depth)
    ]

    def trunk(x3):
        batch, seq, d = x3.shape
        da_tables = _rope_tables(seq, DA_ROT_DIM, DA_QK_DIM)
        mla_tables = _rope_tables(seq, MLA_ROPE_DIM, LANES)
        outs = []
        for bi in range(batch):
            x = x3[bi]
            for l, p in enumerate(layers):
                lambda_init = 0.8 - 0.6 * math.exp(-0.3 * l)
                x = _layer(x, p, lambda_init, da_tables, mla_tables, final_norm_g, is_last=(l == depth - 1))
            outs.append(x)
        return jnp.stack(outs) if batch > 1 else outs[0][None]

    return trunk(x_prompt), trunk(x_sample)
```

```python
import functools
import math

import jax
import jax.numpy as jnp
from jax import lax
from jax.experimental import pallas as pl
from jax.experimental.pallas import tpu as pltpu

D_MODEL = 2048
DA_HEADS = 8
DA_QK_DIM = 64
DA_V_DIM = 2 * DA_QK_DIM
DA_ROT_DIM = DA_QK_DIM // 4
MLA_HEADS = 8
MLA_Q_RANK = 512
MLA_KV_RANK = 512
MLA_NOPE_DIM = 128
MLA_ROPE_DIM = 64
MLA_V_DIM = 128
ROPE_THETA = 500000.0
NORM_EPS = 1e-6
FFN_HIDDEN = -(-8 * D_MODEL // (3 * 256)) * 256
DA_COLS = DA_HEADS * 2 * DA_QK_DIM
MLA_IN_COLS = MLA_Q_RANK + MLA_KV_RANK + MLA_ROPE_DIM
GATE_COLS = 2 * D_MODEL

LANES = 128
MLA_HEAD_BLOCK = 2 * LANES
VMEM_LIMIT_BYTES = 56 * 1024 * 1024
LOG2E = math.log2(math.e)
DA_Q_SCALE = DA_QK_DIM ** -0.5 * LOG2E
MLA_Q_SCALE = (MLA_NOPE_DIM + MLA_ROPE_DIM) ** -0.5 * LOG2E
FIXED_SHIFT_ROW_SUM_LIMIT_PER_KEY = 4.0

BF16 = jnp.bfloat16
F32 = jnp.float32
_NT = (((1,), (1,)), ((), ()))


def _params(n_grid_dims):
    return pltpu.CompilerParams(
        dimension_semantics=("arbitrary",) * n_grid_dims,
        vmem_limit_bytes=VMEM_LIMIT_BYTES,
    )


def _rms(x, g):
    var = jnp.mean(x * x, axis=-1, keepdims=True)
    return x * lax.rsqrt(var + NORM_EPS) * g


def _rope_tables(seq, rot_dim, period):
    half = rot_dim // 2
    pos = jnp.arange(seq, dtype=F32)
    inv_freq = ROPE_THETA ** (-jnp.arange(0, rot_dim, 2, dtype=F32) / rot_dim)
    ang = pos[:, None] * inv_freq[None, :]
    cos, sin = jnp.cos(ang), jnp.sin(ang)
    rest = period - rot_dim
    zeros_half = jnp.zeros((seq, half), F32)
    zeros_rest = jnp.zeros((seq, rest), F32)
    a = jnp.concatenate([cos, cos, jnp.ones((seq, rest), F32)], axis=-1)
    b = jnp.concatenate([-sin, zeros_half, zeros_rest], axis=-1)
    c = jnp.concatenate([zeros_half, sin, zeros_rest], axis=-1)
    reps = LANES // period
    return tuple(jnp.tile(t, (1, reps)) for t in (a, b, c))


def _rope_tile(x, a, b, c, half):
    return x * a + pltpu.roll(x, LANES - half, 1) * b + pltpu.roll(x, half, 1) * c


def _rmsnorm_kernel(x_ref, g_ref, o_ref):
    o_ref[...] = _rms(x_ref[...], g_ref[...]).astype(o_ref.dtype)


def _rmsnorm_bf16(x, g, tm=512):
    s, d = x.shape
    return pl.pallas_call(
        _rmsnorm_kernel,
        grid=(s // tm,),
        in_specs=[pl.BlockSpec((tm, d), lambda i: (i, 0)), pl.BlockSpec((1, d), lambda i: (0, 0))],
        out_specs=pl.BlockSpec((tm, d), lambda i: (i, 0)),
        out_shape=jax.ShapeDtypeStruct((s, d), BF16),
        compiler_params=_params(1),
        name="rmsnorm",
    )(x, g.reshape(1, d))


def _proj_kernel(*refs, mode):
    if mode in ("rope_q", "rope_k"):
        h_ref, w_ref, a_ref, b_ref, c_ref, o_ref = refs
    else:
        h_ref, w_ref, o_ref = refs
    acc = jnp.dot(h_ref[...], w_ref[...], preferred_element_type=F32)
    if mode == "plain":
        o_ref[...] = acc.astype(o_ref.dtype)
        return
    if mode == "sigmoid":
        o_ref[...] = jax.nn.sigmoid(acc)
        return
    a, b, c = a_ref[...], b_ref[...], c_ref[...]
    tm, tn = acc.shape
    first_component = lax.broadcasted_iota(jnp.int32, (tm, LANES), 1) < DA_QK_DIM
    for t in range(tn // LANES):
        cols = slice(t * LANES, (t + 1) * LANES)
        y = _rope_tile(acc[:, cols], a, b, c, DA_ROT_DIM // 2)
        if mode == "rope_k":
            o_ref[:, cols] = y.astype(o_ref.dtype)
        else:
            y = y * DA_Q_SCALE
            o_ref[0, :, cols] = jnp.where(first_component, y, 0.0).astype(o_ref.dtype)
            o_ref[1, :, cols] = jnp.where(first_component, 0.0, y).astype(o_ref.dtype)


def _proj(h, w, mode, tables=(), tm=1024, tn=1024):
    s, k = h.shape
    n = w.shape[1]
    in_specs = [pl.BlockSpec((tm, k), lambda i, j: (i, 0)), pl.BlockSpec((k, tn), lambda i, j: (0, j))]
    in_specs += [pl.BlockSpec((tm, LANES), lambda i, j: (i, 0)) for _ in tables]
    if mode == "rope_q":
        out_shape = jax.ShapeDtypeStruct((2, s, n), BF16)
        out_spec = pl.BlockSpec((2, tm, tn), lambda i, j: (0, i, j))
    else:
        out_shape = jax.ShapeDtypeStruct((s, n), F32 if mode == "sigmoid" else BF16)
        out_spec = pl.BlockSpec((tm, tn), lambda i, j: (i, j))
    return pl.pallas_call(
        functools.partial(_proj_kernel, mode=mode),
        grid=(s // tm, n // tn),
        in_specs=in_specs,
        out_specs=out_spec,
        out_shape=out_shape,
        compiler_params=_params(2),
        name="proj_" + mode,
    )(h, w, *tables)


def _mla_pre_kernel(h_ref, win_ref, gq_ref, wqb_ref, gkv_ref, wkvb_ref, a_ref, b_ref, c_ref,
                    q_ref, k_ref, v_ref):
    a, b, c = a_ref[...], b_ref[...], c_ref[...]
    half = MLA_ROPE_DIM // 2
    lat = jnp.dot(h_ref[...], win_ref[...], preferred_element_type=F32)
    cq = _rms(lat[:, :MLA_Q_RANK], gq_ref[...]).astype(BF16)
    ckv = _rms(lat[:, MLA_Q_RANK:MLA_Q_RANK + MLA_KV_RANK], gkv_ref[...]).astype(BF16)
    k_rope = _rope_tile(lat[:, MLA_Q_RANK + MLA_KV_RANK:], a, b, c, half).astype(BF16)
    q = jnp.dot(cq, wqb_ref[...], preferred_element_type=F32)
    kv = jnp.dot(ckv, wkvb_ref[...], preferred_element_type=F32)
    for hd in range(MLA_HEADS):
        lo = hd * MLA_HEAD_BLOCK
        mid = lo + LANES
        hi = lo + MLA_HEAD_BLOCK
        q_ref[:, lo:mid] = (q[:, lo:mid] * MLA_Q_SCALE).astype(BF16)
        q_ref[:, mid:hi] = (_rope_tile(q[:, mid:hi], a, b, c, half) * MLA_Q_SCALE).astype(BF16)
        k_ref[:, lo:mid] = kv[:, lo:mid].astype(BF16)
        k_ref[:, mid:hi] = k_rope
        v_ref[:, hd * LANES:(hd + 1) * LANES] = kv[:, mid:hi].astype(BF16)


def _mla_pre(h, w_mla_in, gq, wqb, gkv, wkvb, tables, tm=256):
    s, d = h.shape
    qk_cols = MLA_HEADS * MLA_HEAD_BLOCK
    v_cols = MLA_HEADS * MLA_V_DIM
    full = lambda arr: pl.BlockSpec(arr.shape, lambda i: (0, 0))
    row = lambda n: pl.BlockSpec((tm, n), lambda i: (i, 0))
    return pl.pallas_call(
        _mla_pre_kernel,
        grid=(s // tm,),
        in_specs=[row(d), full(w_mla_in), full(gq), full(wqb), full(gkv), full(wkvb),
                  row(LANES), row(LANES), row(LANES)],
        out_specs=[row(qk_cols), row(qk_cols), row(v_cols)],
        out_shape=[jax.ShapeDtypeStruct((s, qk_cols), BF16),
                   jax.ShapeDtypeStruct((s, qk_cols), BF16),
                   jax.ShapeDtypeStruct((s, v_cols), BF16)],
        compiler_params=_params(1),
        name="mla_pre",
    )(h, w_mla_in, gq, wqb, gkv, wkvb, *tables)


def _flash_kernel(*refs, mode, tk, lambda_init):
    if mode == "da":
        (q_ref, k_ref, v_ref, lq1_ref, lk1_ref, lq2_ref, lk2_ref, g_ref, o_ref,
         s_sc, p_sc, alpha_sc, m_sc, acc_sc) = refs
    else:
        q_ref, k_ref, v_ref, o_ref, s_sc, p_sc, alpha_sc, m_sc, acc_sc = refs
    rows = m_sc.shape[0]
    n_kv = k_ref.shape[0] // tk
    q = q_ref[...].reshape(rows, q_ref.shape[-1])
    ones = jnp.ones((tk, LANES), BF16)

    def chunk(j):
        return pl.ds(pl.multiple_of(j * tk, tk), tk)

    def scores(j, slot):
        s_sc[slot] = lax.dot_general(q, k_ref[chunk(j), :], _NT, preferred_element_type=F32)

    def weights(slot, shift):
        s = s_sc[slot]
        for t in range(tk // LANES):
            cols = slice(t * LANES, (t + 1) * LANES)
            p_sc[slot, :, cols] = jnp.exp2((s[:, cols] - shift).astype(BF16))

    def p_times_v_and_ones(j, slot):
        v_and_ones = jnp.concatenate([v_ref[chunk(j), :], ones], axis=1)
        return jnp.dot(p_sc[slot], v_and_ones, preferred_element_type=F32)

    def init_fixed():
        acc_sc[...] = jnp.zeros(acc_sc.shape, F32)
        m_sc[...] = jnp.broadcast_to(jnp.max(s_sc[0], axis=-1, keepdims=True), m_sc.shape)

    def softmax_fixed(slot):
        weights(slot, m_sc[...])

    def accumulate_fixed(j, slot):
        acc_sc[...] += p_times_v_and_ones(j, slot)

    def init_online():
        acc_sc[...] = jnp.zeros(acc_sc.shape, F32)
        m_sc[...] = jnp.full(m_sc.shape, -jnp.inf, F32)

    def softmax_online(slot):
        m_prev = m_sc[...]
        m_new = jnp.maximum(m_prev, jnp.max(s_sc[slot], axis=-1, keepdims=True))
        alpha_sc[slot] = jnp.exp2(m_prev - m_new)
        weights(slot, m_new)
        m_sc[...] = m_new

    def accumulate_online(j, slot):
        pv = p_times_v_and_ones(j, slot)
        alpha = alpha_sc[slot]
        acc_sc[:, :LANES] = alpha * acc_sc[:, :LANES] + pv[:, :LANES]
        acc_sc[:, LANES:] = alpha * acc_sc[:, LANES:] + pv[:, LANES:]

    def run_pipeline(init, softmax, accumulate, unroll):
        scores(0, 0)
        scores(1, 1)
        init()
        softmax(0)

        def kv_pair(i, carry):
            t = 2 * i + 2
            scores(t, 0)
            softmax(1)
            accumulate(t - 2, 0)
            scores(t + 1, 1)
            softmax(0)
            accumulate(t - 1, 1)
            return carry

        lax.fori_loop(0, n_kv // 2 - 1, kv_pair, 0, unroll=unroll)
        softmax(1)
        accumulate(n_kv - 2, 0)
        accumulate(n_kv - 1, 1)

    run_pipeline(init_fixed, softmax_fixed, accumulate_fixed, unroll=4)
    row_sum_limit = FIXED_SHIFT_ROW_SUM_LIMIT_PER_KEY * k_ref.shape[0]

    @pl.when(jnp.logical_not(jnp.max(acc_sc[:, LANES:]) <= row_sum_limit))
    def _():
        run_pipeline(init_online, softmax_online, accumulate_online, unroll=1)

    o = acc_sc[:, :LANES] / acc_sc[:, LANES:]
    if mode == "da":
        tq = rows // 2
        lam = (jnp.exp(jnp.sum(lq1_ref[...] * lk1_ref[...])) - jnp.exp(jnp.sum(lq2_ref[...] * lk2_ref[...]))
               + lambda_init)
        diff = o[:tq] - lam * o[tq:]
        o = _rms(diff, g_ref[...]) * (1.0 - lambda_init)
    o_ref[...] = o.astype(o_ref.dtype)


def _flash(q, k, v, mode, extra=(), lambda_init=0.0, tq=512, tk=512):
    s = k.shape[0]
    heads = v.shape[1] // LANES
    dk = k.shape[1] // heads
    if mode == "da":
        rows = 2 * tq
        q_spec = pl.BlockSpec((2, tq, dk), lambda h, i: (0, i, h))
    else:
        rows = tq
        q_spec = pl.BlockSpec((tq, dk), lambda h, i: (i, h))
    in_specs = [q_spec,
                pl.BlockSpec((s, dk), lambda h, i: (0, h)),
                pl.BlockSpec((s, LANES), lambda h, i: (0, h))]
    in_specs += [pl.BlockSpec(e.shape, lambda h, i: (0, 0)) for e in extra]
    return pl.pallas_call(
        functools.partial(_flash_kernel, mode=mode, tk=tk, lambda_init=lambda_init),
        grid=(heads, s // tq),
        in_specs=in_specs,
        out_specs=pl.BlockSpec((tq, LANES), lambda h, i: (i, h)),
        out_shape=jax.ShapeDtypeStruct((s, heads * LANES), BF16),
        scratch_shapes=[pltpu.VMEM((2, rows, tk), F32), pltpu.VMEM((2, rows, tk), BF16),
                        pltpu.VMEM((2, rows, LANES), F32), pltpu.VMEM((rows, LANES), F32),
                        pltpu.VMEM((rows, 2 * LANES), F32)],
        compiler_params=_params(2),
        name="flash_" + mode,
    )(q, k, v, *extra)


def _merge_kernel(oda_ref, omla_ref, gda_ref, gmla_ref, wda_ref, wmla_ref, wo_ref, x_ref, o_ref, acc_sc):
    c = pl.program_id(1)

    @pl.when(c == 0)
    def _():
        acc_sc[...] = jnp.zeros(acc_sc.shape, F32)

    b_da = jnp.dot(oda_ref[...], wda_ref[...], preferred_element_type=F32)
    b_mla = jnp.dot(omla_ref[...], wmla_ref[...], preferred_element_type=F32)
    merged = gda_ref[...] * b_da + gmla_ref[...] * b_mla
    acc_sc[...] += jnp.dot(merged.astype(BF16), wo_ref[...], preferred_element_type=F32)

    @pl.when(c == pl.num_programs(1) - 1)
    def _():
        o_ref[...] = x_ref[...] + acc_sc[...]


def _merge(o_da, o_mla, gates, w_da, w_mla, w_out, x, tm=512, tc=1024):
    s, d = x.shape
    n_c = d // tc
    return pl.pallas_call(
        _merge_kernel,
        grid=(s // tm, n_c),
        in_specs=[pl.BlockSpec((tm, o_da.shape[1]), lambda i, c: (i, 0)),
                  pl.BlockSpec((tm, o_mla.shape[1]), lambda i, c: (i, 0)),
                  pl.BlockSpec((tm, tc), lambda i, c: (i, c)),
                  pl.BlockSpec((tm, tc), lambda i, c: (i, c + n_c)),
                  pl.BlockSpec((w_da.shape[0], tc), lambda i, c: (0, c)),
                  pl.BlockSpec((w_mla.shape[0], tc), lambda i, c: (0, c)),
                  pl.BlockSpec((tc, d), lambda i, c: (c, 0)),
                  pl.BlockSpec((tm, d), lambda i, c: (i, 0))],
        out_specs=pl.BlockSpec((tm, d), lambda i, c: (i, 0)),
        out_shape=jax.ShapeDtypeStruct((s, d), F32),
        scratch_shapes=[pltpu.VMEM((tm, d), F32)],
        compiler_params=_params(2),
        name="merge",
    )(o_da, o_mla, gates, gates, w_da, w_mla, w_out, x)


def _ffn_kernel(x_ref, g_ref, wg_ref, wu_ref, wd_ref, gf_ref, o_ref, h_sc, acc_sc, *, final_norm):
    f = pl.program_id(1)

    @pl.when(f == 0)
    def _():
        h_sc[...] = _rms(x_ref[...], g_ref[...]).astype(h_sc.dtype)
        acc_sc[...] = jnp.zeros(acc_sc.shape, F32)

    h = h_sc[...]
    gate = jnp.dot(h, wg_ref[...], preferred_element_type=F32)
    up = jnp.dot(h, wu_ref[...], preferred_element_type=F32)
    act = (jax.nn.silu(gate) * up).astype(BF16)
    acc_sc[...] += jnp.dot(act, wd_ref[...], preferred_element_type=F32)

    @pl.when(f == pl.num_programs(1) - 1)
    def _():
        y = x_ref[...] + acc_sc[...]
        o_ref[...] = _rms(y, gf_ref[...]) if final_norm else y


def _ffn(x, g, w_gate, w_up, w_down, g_final, final_norm, tm=512, tf=512):
    s, d = x.shape
    hidden = w_gate.shape[1]
    return pl.pallas_call(
        functools.partial(_ffn_kernel, final_norm=final_norm),
        grid=(s // tm, hidden // tf),
        in_specs=[pl.BlockSpec((tm, d), lambda i, f: (i, 0)),
                  pl.BlockSpec((1, d), lambda i, f: (0, 0)),
                  pl.BlockSpec((d, tf), lambda i, f: (0, f)),
                  pl.BlockSpec((d, tf), lambda i, f: (0, f)),
                  pl.BlockSpec((tf, d), lambda i, f: (f, 0)),
                  pl.BlockSpec((1, d), lambda i, f: (0, 0))],
        out_specs=pl.BlockSpec((tm, d), lambda i, f: (i, 0)),
        out_shape=jax.ShapeDtypeStruct((s, d), F32),
        scratch_shapes=[pltpu.VMEM((tm, d), BF16), pltpu.VMEM((tm, d), F32)],
        compiler_params=_params(2),
        name="ffn",
    )(x, g.reshape(1, d), w_gate, w_up, w_down, g_final.reshape(1, d))


def _prepare_layer(l, attn_norm_g, w_in, da_lambda_q1, da_lambda_k1, da_lambda_q2, da_lambda_k2, da_subln_g,
                   mla_q_norm_g, mla_w_q_b, mla_kv_norm_g, mla_w_kv_b, w_branch_da, w_branch_mla, w_out,
                   ffn_norm_g, w_gate, w_up, w_down):
    w = w_in[l]
    cuts = [0, DA_COLS, 2 * DA_COLS, 3 * DA_COLS, 3 * DA_COLS + MLA_IN_COLS]
    pad = jnp.zeros((D_MODEL, LANES - MLA_ROPE_DIM), F32)
    head_dim = MLA_NOPE_DIM + MLA_ROPE_DIM
    wqb = mla_w_q_b[l].reshape(MLA_Q_RANK, MLA_HEADS, head_dim)
    wqb = jnp.pad(wqb, ((0, 0), (0, 0), (0, MLA_HEAD_BLOCK - head_dim)))
    return dict(
        attn_g=attn_norm_g[l],
        wq=w[:, cuts[0]:cuts[1]].astype(BF16),
        wk=w[:, cuts[1]:cuts[2]].astype(BF16),
        wv=w[:, cuts[2]:cuts[3]].astype(BF16),
        w_mla_in=jnp.concatenate([w[:, cuts[3]:cuts[4]], pad], axis=1).astype(BF16),
        w_gates=w[:, cuts[4]:].astype(BF16),
        lambdas=tuple(p[l].reshape(1, DA_QK_DIM) for p in (da_lambda_q1, da_lambda_k1, da_lambda_q2, da_lambda_k2)),
        subln_g=da_subln_g[l].reshape(1, DA_V_DIM),
        gq=mla_q_norm_g[l].reshape(1, MLA_Q_RANK),
        wqb=wqb.reshape(MLA_Q_RANK, MLA_HEADS * MLA_HEAD_BLOCK).astype(BF16),
        gkv=mla_kv_norm_g[l].reshape(1, MLA_KV_RANK),
        wkvb=mla_w_kv_b[l].astype(BF16),
        w_branch_da=w_branch_da[l].astype(BF16),
        w_branch_mla=w_branch_mla[l].astype(BF16),
        w_out=w_out[l].astype(BF16),
        ffn_g=ffn_norm_g[l],
        w_gate=w_gate[l].astype(BF16),
        w_up=w_up[l].astype(BF16),
        w_down=w_down[l].astype(BF16),
    )


def _layer(x, p, lambda_init, da_tables, mla_tables, final_norm_g, is_last):
    h = _rmsnorm_bf16(x, p["attn_g"])
    q_da = _proj(h, p["wq"], "rope_q", da_tables)
    k_da = _proj(h, p["wk"], "rope_k", da_tables)
    v_da = _proj(h, p["wv"], "plain")
    gates = _proj(h, p["w_gates"], "sigmoid")
    q_mla, k_mla, v_mla = _mla_pre(h, p["w_mla_in"], p["gq"], p["wqb"], p["gkv"], p["wkvb"], mla_tables)
    o_da = _flash(q_da, k_da, v_da, "da", extra=p["lambdas"] + (p["subln_g"],), lambda_init=lambda_init)
    o_mla = _flash(q_mla, k_mla, v_mla, "mla", tq=1024)
    x = _merge(o_da, o_mla, gates, p["w_branch_da"], p["w_branch_mla"], p["w_out"], x)
    return _ffn(x, p["ffn_g"], p["w_gate"], p["w_up"], p["w_down"], final_norm_g, final_norm=is_last)


def kernel(x_prompt, x_sample, attn_norm_g, w_in, da_lambda_q1, da_lambda_k1, da_lambda_q2, da_lambda_k2, da_subln_g, mla_q_norm_g, mla_w_q_b, mla_kv_norm_g, mla_w_kv_b, w_branch_da, w_branch_mla, w_out, ffn_norm_g, w_gate, w_up, w_down, final_norm_g):
    depth = w_in.shape[0]
    layers = [
        _prepare_layer(l, attn_norm_g, w_in, da_lambda_q1, da_lambda_k1, da_lambda_q2, da_lambda_k2, da_subln_g,
                       mla_q_norm_g, mla_w_q_b, mla_kv_norm_g, mla_w_kv_b, w_branch_da, w_branch_mla, w_out,
                       ffn_norm_g, w_gate, w_up, w_down)
        for l in range(depth)
    ]

    def trunk(x3):
        batch, seq, d = x3.shape
        da_tables = _rope_tables(seq, DA_ROT_DIM, DA_QK_DIM)
        mla_tables = _rope_tables(seq, MLA_ROPE_DIM, LANES)
        outs = []
        for bi in range(batch):
            x = x3[bi]
            for l, p in enumerate(layers):
                lambda_init = 0.8 - 0.6 * math.exp(-0.3 * l)
                x = _layer(x, p, lambda_init, da_tables, mla_tables, final_norm_g, is_last=(l == depth - 1))
            outs.append(x)
        return jnp.stack(outs) if batch > 1 else outs[0][None]

    return trunk(x_prompt), trunk(x_sample)
```

```python
import functools
import math

import jax
import jax.numpy as jnp
from jax import lax
from jax.experimental import pallas as pl
from jax.experimental.pallas import tpu as pltpu

D_MODEL = 2048
DA_HEADS = 8
DA_QK_DIM = 64
DA_V_DIM = 2 * DA_QK_DIM
DA_ROT_DIM = DA_QK_DIM // 4
MLA_HEADS = 8
MLA_Q_RANK = 512
MLA_KV_RANK = 512
MLA_NOPE_DIM = 128
MLA_ROPE_DIM = 64
MLA_V_DIM = 128
ROPE_THETA = 500000.0
NORM_EPS = 1e-6
FFN_HIDDEN = -(-8 * D_MODEL // (3 * 256)) * 256
DA_COLS = DA_HEADS * 2 * DA_QK_DIM
MLA_IN_COLS = MLA_Q_RANK + MLA_KV_RANK + MLA_ROPE_DIM
GATE_COLS = 2 * D_MODEL

LANES = 128
MLA_HEAD_BLOCK = 2 * LANES
VMEM_LIMIT_BYTES = 56 * 1024 * 1024
LOG2E = math.log2(math.e)
DA_Q_SCALE = DA_QK_DIM ** -0.5 * LOG2E
MLA_Q_SCALE = (MLA_NOPE_DIM + MLA_ROPE_DIM) ** -0.5 * LOG2E
FIXED_SHIFT_ROW_SUM_LIMIT_PER_KEY = 4.0

BF16 = jnp.bfloat16
F32 = jnp.float32
_NT = (((1,), (1,)), ((), ()))


def _params(n_grid_dims):
    return pltpu.CompilerParams(
        dimension_semantics=("arbitrary",) * n_grid_dims,
        vmem_limit_bytes=VMEM_LIMIT_BYTES,
    )


def _rms(x, g):
    var = jnp.mean(x * x, axis=-1, keepdims=True)
    return x * lax.rsqrt(var + NORM_EPS) * g


def _rope_tables(seq, rot_dim, period):
    half = rot_dim // 2
    pos = jnp.arange(seq, dtype=F32)
    inv_freq = ROPE_THETA ** (-jnp.arange(0, rot_dim, 2, dtype=F32) / rot_dim)
    ang = pos[:, None] * inv_freq[None, :]
    cos, sin = jnp.cos(ang), jnp.sin(ang)
    rest = period - rot_dim
    zeros_half = jnp.zeros((seq, half), F32)
    zeros_rest = jnp.zeros((seq, rest), F32)
    a = jnp.concatenate([cos, cos, jnp.ones((seq, rest), F32)], axis=-1)
    b = jnp.concatenate([-sin, zeros_half, zeros_rest], axis=-1)
    c = jnp.concatenate([zeros_half, sin, zeros_rest], axis=-1)
    reps = LANES // period
    return tuple(jnp.tile(t, (1, reps)) for t in (a, b, c))


def _rope_tile(x, a, b, c, half):
    return x * a + pltpu.roll(x, LANES - half, 1) * b + pltpu.roll(x, half, 1) * c


def _rmsnorm_kernel(x_ref, g_ref, o_ref):
    o_ref[...] = _rms(x_ref[...], g_ref[...]).astype(o_ref.dtype)


def _rmsnorm_bf16(x, g, tm=512):
    s, d = x.shape
    return pl.pallas_call(
        _rmsnorm_kernel,
        grid=(s // tm,),
        in_specs=[pl.BlockSpec((tm, d), lambda i: (i, 0)), pl.BlockSpec((1, d), lambda i: (0, 0))],
        out_specs=pl.BlockSpec((tm, d), lambda i: (i, 0)),
        out_shape=jax.ShapeDtypeStruct((s, d), BF16),
        compiler_params=_params(1),
        name="rmsnorm",
    )(x, g.reshape(1, d))


def _proj_kernel(*refs, mode):
    if mode in ("rope_q", "rope_k"):
        h_ref, w_ref, a_ref, b_ref, c_ref, o_ref = refs
    else:
        h_ref, w_ref, o_ref = refs
    acc = jnp.dot(h_ref[...], w_ref[...], preferred_element_type=F32)
    if mode == "plain":
        o_ref[...] = acc.astype(o_ref.dtype)
        return
    if mode == "sigmoid":
        o_ref[...] = jax.nn.sigmoid(acc)
        return
    a, b, c = a_ref[...], b_ref[...], c_ref[...]
    tm, tn = acc.shape
    first_component = lax.broadcasted_iota(jnp.int32, (tm, LANES), 1) < DA_QK_DIM
    for t in range(tn // LANES):
        cols = slice(t * LANES, (t + 1) * LANES)
        y = _rope_tile(acc[:, cols], a, b, c, DA_ROT_DIM // 2)
        if mode == "rope_k":
            o_ref[:, cols] = y.astype(o_ref.dtype)
        else:
            y = y * DA_Q_SCALE
            o_ref[0, :, cols] = jnp.where(first_component, y, 0.0).astype(o_ref.dtype)
            o_ref[1, :, cols] = jnp.where(first_component, 0.0, y).astype(o_ref.dtype)


def _proj(h, w, mode, tables=(), tm=1024, tn=1024):
    s, k = h.shape
    n = w.shape[1]
    in_specs = [pl.BlockSpec((tm, k), lambda i, j: (i, 0)), pl.BlockSpec((k, tn), lambda i, j: (0, j))]
    in_specs += [pl.BlockSpec((tm, LANES), lambda i, j: (i, 0)) for _ in tables]
    if mode == "rope_q":
        out_shape = jax.ShapeDtypeStruct((2, s, n), BF16)
        out_spec = pl.BlockSpec((2, tm, tn), lambda i, j: (0, i, j))
    else:
        out_shape = jax.ShapeDtypeStruct((s, n), F32 if mode == "sigmoid" else BF16)
        out_spec = pl.BlockSpec((tm, tn), lambda i, j: (i, j))
    return pl.pallas_call(
        functools.partial(_proj_kernel, mode=mode),
        grid=(s // tm, n // tn),
        in_specs=in_specs,
        out_specs=out_spec,
        out_shape=out_shape,
        compiler_params=_params(2),
        name="proj_" + mode,
    )(h, w, *tables)


def _mla_pre_kernel(h_ref, win_ref, gq_ref, wqb_ref, gkv_ref, wkvb_ref, a_ref, b_ref, c_ref,
                    q_ref, k_ref, v_ref):
    a, b, c = a_ref[...], b_ref[...], c_ref[...]
    half = MLA_ROPE_DIM // 2
    lat = jnp.dot(h_ref[...], win_ref[...], preferred_element_type=F32)
    cq = _rms(lat[:, :MLA_Q_RANK], gq_ref[...]).astype(BF16)
    ckv = _rms(lat[:, MLA_Q_RANK:MLA_Q_RANK + MLA_KV_RANK], gkv_ref[...]).astype(BF16)
    k_rope = _rope_tile(lat[:, MLA_Q_RANK + MLA_KV_RANK:], a, b, c, half).astype(BF16)
    q = jnp.dot(cq, wqb_ref[...], preferred_element_type=F32)
    kv = jnp.dot(ckv, wkvb_ref[...], preferred_element_type=F32)
    for hd in range(MLA_HEADS):
        lo = hd * MLA_HEAD_BLOCK
        mid = lo + LANES
        hi = lo + MLA_HEAD_BLOCK
        q_ref[:, lo:mid] = (q[:, lo:mid] * MLA_Q_SCALE).astype(BF16)
        q_ref[:, mid:hi] = (_rope_tile(q[:, mid:hi], a, b, c, half) * MLA_Q_SCALE).astype(BF16)
        k_ref[:, lo:mid] = kv[:, lo:mid].astype(BF16)
        k_ref[:, mid:hi] = k_rope
        v_ref[:, hd * LANES:(hd + 1) * LANES] = kv[:, mid:hi].astype(BF16)


def _mla_pre(h, w_mla_in, gq, wqb, gkv, wkvb, tables, tm=256):
    s, d = h.shape
    qk_cols = MLA_HEADS * MLA_HEAD_BLOCK
    v_cols = MLA_HEADS * MLA_V_DIM
    full = lambda arr: pl.BlockSpec(arr.shape, lambda i: (0, 0))
    row = lambda n: pl.BlockSpec((tm, n), lambda i: (i, 0))
    return pl.pallas_call(
        _mla_pre_kernel,
        grid=(s // tm,),
        in_specs=[row(d), full(w_mla_in), full(gq), full(wqb), full(gkv), full(wkvb),
                  row(LANES), row(LANES), row(LANES)],
        out_specs=[row(qk_cols), row(qk_cols), row(v_cols)],
        out_shape=[jax.ShapeDtypeStruct((s, qk_cols), BF16),
                   jax.ShapeDtypeStruct((s, qk_cols), BF16),
                   jax.ShapeDtypeStruct((s, v_cols), BF16)],
        compiler_params=_params(1),
        name="mla_pre",
    )(h, w_mla_in, gq, wqb, gkv, wkvb, *tables)


def _flash_kernel(*refs, mode, tk, lambda_init):
    if mode == "da":
        (q_ref, k_ref, v_ref, lq1_ref, lk1_ref, lq2_ref, lk2_ref, g_ref, o_ref,
         s_sc, p_sc, alpha_sc, m_sc, acc_sc) = refs
    else:
        q_ref, k_ref, v_ref, o_ref, s_sc, p_sc, alpha_sc, m_sc, acc_sc = refs
    rows = m_sc.shape[0]
    n_kv = k_ref.shape[0] // tk
    q = q_ref[...].reshape(rows, q_ref.shape[-1])
    ones = jnp.ones((tk, LANES), BF16)

    def chunk(j):
        return pl.ds(pl.multiple_of(j * tk, tk), tk)

    def scores(j, slot):
        s_sc[slot] = lax.dot_general(q, k_ref[chunk(j), :], _NT, preferred_element_type=F32)

    def weights(slot, shift):
        s = s_sc[slot]
        for t in range(tk // LANES):
            cols = slice(t * LANES, (t + 1) * LANES)
            p_sc[slot, :, cols] = jnp.exp2((s[:, cols] - shift).astype(BF16))

    def p_times_v_and_ones(j, slot):
        v_and_ones = jnp.concatenate([v_ref[chunk(j), :], ones], axis=1)
        return jnp.dot(p_sc[slot], v_and_ones, preferred_element_type=F32)

    def init_fixed():
        acc_sc[...] = jnp.zeros(acc_sc.shape, F32)
        m_sc[...] = jnp.broadcast_to(jnp.max(s_sc[0], axis=-1, keepdims=True), m_sc.shape)

    def softmax_fixed(slot):
        weights(slot, m_sc[...])

    def accumulate_fixed(j, slot):
        acc_sc[...] += p_times_v_and_ones(j, slot)

    def init_online():
        acc_sc[...] = jnp.zeros(acc_sc.shape, F32)
        m_sc[...] = jnp.full(m_sc.shape, -jnp.inf, F32)

    def softmax_online(slot):
        m_prev = m_sc[...]
        m_new = jnp.maximum(m_prev, jnp.max(s_sc[slot], axis=-1, keepdims=True))
        alpha_sc[slot] = jnp.exp2(m_prev - m_new)
        weights(slot, m_new)
        m_sc[...] = m_new

    def accumulate_online(j, slot):
        pv = p_times_v_and_ones(j, slot)
        alpha = alpha_sc[slot]
        acc_sc[:, :LANES] = alpha * acc_sc[:, :LANES] + pv[:, :LANES]
        acc_sc[:, LANES:] = alpha * acc_sc[:, LANES:] + pv[:, LANES:]

    def run_pipeline(init, softmax, accumulate, unroll):
        scores(0, 0)
        scores(1, 1)
        init()
        softmax(0)

        def kv_pair(i, carry):
            t = 2 * i + 2
            scores(t, 0)
            softmax(1)
            accumulate(t - 2, 0)
            scores(t + 1, 1)
            softmax(0)
            accumulate(t - 1, 1)
            return carry

        lax.fori_loop(0, n_kv // 2 - 1, kv_pair, 0, unroll=unroll)
        softmax(1)
        accumulate(n_kv - 2, 0)
        accumulate(n_kv - 1, 1)

    run_pipeline(init_fixed, softmax_fixed, accumulate_fixed, unroll=4)
    row_sum_limit = FIXED_SHIFT_ROW_SUM_LIMIT_PER_KEY * k_ref.shape[0]

    @pl.when(jnp.logical_not(jnp.max(acc_sc[:, LANES:]) <= row_sum_limit))
    def _():
        run_pipeline(init_online, softmax_online, accumulate_online, unroll=1)

    o = acc_sc[:, :LANES] / acc_sc[:, LANES:]
    if mode == "da":
        tq = rows // 2
        lam = (jnp.exp(jnp.sum(lq1_ref[...] * lk1_ref[...])) - jnp.exp(jnp.sum(lq2_ref[...] * lk2_ref[...]))
               + lambda_init)
        diff = o[:tq] - lam * o[tq:]
        o = _rms(diff, g_ref[...]) * (1.0 - lambda_init)
    o_ref[...] = o.astype(o_ref.dtype)


def _flash(q, k, v, mode, extra=(), lambda_init=0.0, tq=512, tk=512):
    s = k.shape[0]
    heads = v.shape[1] // LANES
    dk = k.shape[1] // heads
    if mode == "da":
        rows = 2 * tq
        q_spec = pl.BlockSpec((2, tq, dk), lambda h, i: (0, i, h))
    else:
        rows = tq
        q_spec = pl.BlockSpec((tq, dk), lambda h, i: (i, h))
    in_specs = [q_spec,
                pl.BlockSpec((s, dk), lambda h, i: (0, h)),
                pl.BlockSpec((s, LANES), lambda h, i: (0, h))]
    in_specs += [pl.BlockSpec(e.shape, lambda h, i: (0, 0)) for e in extra]
    return pl.pallas_call(
        functools.partial(_flash_kernel, mode=mode, tk=tk, lambda_init=lambda_init),
        grid=(heads, s // tq),
        in_specs=in_specs,
        out_specs=pl.BlockSpec((tq, LANES), lambda h, i: (i, h)),
        out_shape=jax.ShapeDtypeStruct((s, heads * LANES), BF16),
        scratch_shapes=[pltpu.VMEM((2, rows, tk), F32), pltpu.VMEM((2, rows, tk), BF16),
                        pltpu.VMEM((2, rows, LANES), F32), pltpu.VMEM((rows, LANES), F32),
                        pltpu.VMEM((rows, 2 * LANES), F32)],
        compiler_params=_params(2),
        name="flash_" + mode,
    )(q, k, v, *extra)


def _merge_kernel(oda_ref, omla_ref, gda_ref, gmla_ref, wda_ref, wmla_ref, wo_ref, x_ref, o_ref):
    b_da = jnp.dot(oda_ref[...], wda_ref[...], preferred_element_type=F32)
    b_mla = jnp.dot(omla_ref[...], wmla_ref[...], preferred_element_type=F32)
    merged = gda_ref[...] * b_da + gmla_ref[...] * b_mla
    o_ref[...] = x_ref[...] + jnp.dot(merged.astype(BF16), wo_ref[...], preferred_element_type=F32)


def _merge(o_da, o_mla, gates, w_da, w_mla, w_out, x, tm=256):
    s, d = x.shape
    resident = lambda arr: pl.BlockSpec(arr.shape, lambda i: (0, 0), pipeline_mode=pl.Buffered(1))
    return pl.pallas_call(
        _merge_kernel,
        grid=(s // tm,),
        in_specs=[pl.BlockSpec((tm, o_da.shape[1]), lambda i: (i, 0)),
                  pl.BlockSpec((tm, o_mla.shape[1]), lambda i: (i, 0)),
                  pl.BlockSpec((tm, d), lambda i: (i, 0)),
                  pl.BlockSpec((tm, d), lambda i: (i, 1)),
                  resident(w_da), resident(w_mla), resident(w_out),
                  pl.BlockSpec((tm, d), lambda i: (i, 0))],
        out_specs=pl.BlockSpec((tm, d), lambda i: (i, 0)),
        out_shape=jax.ShapeDtypeStruct((s, d), F32),
        compiler_params=_params(1),
        name="merge",
    )(o_da, o_mla, gates, gates, w_da, w_mla, w_out, x)


def _ffn_kernel(x_ref, g_ref, wg_ref, wu_ref, wd_ref, gf_ref, o_ref, h_sc, acc_sc, *, final_norm):
    f = pl.program_id(1)

    @pl.when(f == 0)
    def _():
        h_sc[...] = _rms(x_ref[...], g_ref[...]).astype(h_sc.dtype)
        acc_sc[...] = jnp.zeros(acc_sc.shape, F32)

    h = h_sc[...]
    gate = jnp.dot(h, wg_ref[...], preferred_element_type=F32)
    up = jnp.dot(h, wu_ref[...], preferred_element_type=F32)
    act = (jax.nn.silu(gate) * up).astype(BF16)
    acc_sc[...] += jnp.dot(act, wd_ref[...], preferred_element_type=F32)

    @pl.when(f == pl.num_programs(1) - 1)
    def _():
        y = x_ref[...] + acc_sc[...]
        o_ref[...] = _rms(y, gf_ref[...]) if final_norm else y


def _ffn(x, g, w_gate, w_up, w_down, g_final, final_norm, tm=512, tf=512):
    s, d = x.shape
    hidden = w_gate.shape[1]
    return pl.pallas_call(
        functools.partial(_ffn_kernel, final_norm=final_norm),
        grid=(s // tm, hidden // tf),
        in_specs=[pl.BlockSpec((tm, d), lambda i, f: (i, 0)),
                  pl.BlockSpec((1, d), lambda i, f: (0, 0)),
                  pl.BlockSpec((d, tf), lambda i, f: (0, f)),
                  pl.BlockSpec((d, tf), lambda i, f: (0, f)),
                  pl.BlockSpec((tf, d), lambda i, f: (f, 0)),
                  pl.BlockSpec((1, d), lambda i, f: (0, 0))],
        out_specs=pl.BlockSpec((tm, d), lambda i, f: (i, 0)),
        out_shape=jax.ShapeDtypeStruct((s, d), F32),
        scratch_shapes=[pltpu.VMEM((tm, d), BF16), pltpu.VMEM((tm, d), F32)],
        compiler_params=_params(2),
        name="ffn",
    )(x, g.reshape(1, d), w_gate, w_up, w_down, g_final.reshape(1, d))


def _prepare_layer(l, attn_norm_g, w_in, da_lambda_q1, da_lambda_k1, da_lambda_q2, da_lambda_k2, da_subln_g,
                   mla_q_norm_g, mla_w_q_b, mla_kv_norm_g, mla_w_kv_b, w_branch_da, w_branch_mla, w_out,
                   ffn_norm_g, w_gate, w_up, w_down):
    w = w_in[l]
    cuts = [0, DA_COLS, 2 * DA_COLS, 3 * DA_COLS, 3 * DA_COLS + MLA_IN_COLS]
    pad = jnp.zeros((D_MODEL, LANES - MLA_ROPE_DIM), F32)
    head_dim = MLA_NOPE_DIM + MLA_ROPE_DIM
    wqb = mla_w_q_b[l].reshape(MLA_Q_RANK, MLA_HEADS, head_dim)
    wqb = jnp.pad(wqb, ((0, 0), (0, 0), (0, MLA_HEAD_BLOCK - head_dim)))
    return dict(
        attn_g=attn_norm_g[l],
        wq=w[:, cuts[0]:cuts[1]].astype(BF16),
        wk=w[:, cuts[1]:cuts[2]].astype(BF16),
        wv=w[:, cuts[2]:cuts[3]].astype(BF16),
        w_mla_in=jnp.concatenate([w[:, cuts[3]:cuts[4]], pad], axis=1).astype(BF16),
        w_gates=w[:, cuts[4]:].astype(BF16),
        lambdas=tuple(p[l].reshape(1, DA_QK_DIM) for p in (da_lambda_q1, da_lambda_k1, da_lambda_q2, da_lambda_k2)),
        subln_g=da_subln_g[l].reshape(1, DA_V_DIM),
        gq=mla_q_norm_g[l].reshape(1, MLA_Q_RANK),
        wqb=wqb.reshape(MLA_Q_RANK, MLA_HEADS * MLA_HEAD_BLOCK).astype(BF16),
        gkv=mla_kv_norm_g[l].reshape(1, MLA_KV_RANK),
        wkvb=mla_w_kv_b[l].astype(BF16),
        w_branch_da=w_branch_da[l].astype(BF16),
        w_branch_mla=w_branch_mla[l].astype(BF16),
        w_out=w_out[l].astype(BF16),
        ffn_g=ffn_norm_g[l],
        w_gate=w_gate[l].astype(BF16),
        w_up=w_up[l].astype(BF16),
        w_down=w_down[l].astype(BF16),
    )


def _layer(x, p, lambda_init, da_tables, mla_tables, final_norm_g, is_last):
    h = _rmsnorm_bf16(x, p["attn_g"])
    q_da = _proj(h, p["wq"], "rope_q", da_tables)
    k_da = _proj(h, p["wk"], "rope_k", da_tables)
    v_da = _proj(h, p["wv"], "plain")
    gates = _proj(h, p["w_gates"], "sigmoid")
    q_mla, k_mla, v_mla = _mla_pre(h, p["w_mla_in"], p["gq"], p["wqb"], p["gkv"], p["wkvb"], mla_tables)
    o_da = _flash(q_da, k_da, v_da, "da", extra=p["lambdas"] + (p["subln_g"],), lambda_init=lambda_init)
    o_mla = _flash(q_mla, k_mla, v_mla, "mla", tq=1024)
    x = _merge(o_da, o_mla, gates, p["w_branch_da"], p["w_branch_mla"], p["w_out"], x)
    return _ffn(x, p["ffn_g"], p["w_gate"], p["w_up"], p["w_down"], final_norm_g, final_norm=is_last)


def kernel(x_prompt, x_sample, attn_norm_g, w_in, da_lambda_q1, da_lambda_k1, da_lambda_q2, da_lambda_k2, da_subln_g, mla_q_norm_g, mla_w_q_b, mla_kv_norm_g, mla_w_kv_b, w_branch_da, w_branch_mla, w_out, ffn_norm_g, w_gate, w_up, w_down, final_norm_g):
    depth = w_in.shape[0]
    layers = [
        _prepare_layer(l, attn_norm_g, w_in, da_lambda_q1, da_lambda_k1, da_lambda_q2, da_lambda_k2, da_subln_g,
                       mla_q_norm_g, mla_w_q_b, mla_kv_norm_g, mla_w_kv_b, w_branch_da, w_branch_mla, w_out,
                       ffn_norm_g, w_gate, w_up, w_down)
        for l in range(depth)
    ]

    max_seq = max(x_prompt.shape[1], x_sample.shape[1])
    da_tables = _rope_tables(max_seq, DA_ROT_DIM, DA_QK_DIM)
    mla_tables = _rope_tables(max_seq, MLA_ROPE_DIM, LANES)

    def trunk(x3):
        batch, seq, d = x3.shape
        x = x3.reshape(batch * seq, d)
        outs = []
        for bi in range(batch):
            xb = x if batch == 1 else lax.slice_in_dim(x, bi * seq, (bi + 1) * seq)
            for l, p in enumerate(layers):
                lambda_init = 0.8 - 0.6 * math.exp(-0.3 * l)
                xb = _layer(xb, p, lambda_init, da_tables, mla_tables, final_norm_g, is_last=(l == depth - 1))
            outs.append(xb)
        out = outs[0] if batch == 1 else jnp.concatenate(outs)
        return out.reshape(batch, seq, d)

    return trunk(x_prompt), trunk(x_sample)
```

```python
import functools
import math

import jax
import jax.numpy as jnp
from jax import lax
from jax.experimental import pallas as pl
from jax.experimental.pallas import tpu as pltpu

D_MODEL = 2048
DA_HEADS = 8
DA_QK_DIM = 64
DA_V_DIM = 2 * DA_QK_DIM
DA_ROT_DIM = DA_QK_DIM // 4
MLA_HEADS = 8
MLA_Q_RANK = 512
MLA_KV_RANK = 512
MLA_NOPE_DIM = 128
MLA_ROPE_DIM = 64
MLA_V_DIM = 128
ROPE_THETA = 500000.0
NORM_EPS = 1e-6
FFN_HIDDEN = -(-8 * D_MODEL // (3 * 256)) * 256
DA_COLS = DA_HEADS * 2 * DA_QK_DIM
MLA_IN_COLS = MLA_Q_RANK + MLA_KV_RANK + MLA_ROPE_DIM
GATE_COLS = 2 * D_MODEL

LANES = 128
MLA_HEAD_BLOCK = 2 * LANES
VMEM_LIMIT_BYTES = 56 * 1024 * 1024
LOG2E = math.log2(math.e)
DA_Q_SCALE = DA_QK_DIM ** -0.5 * LOG2E
MLA_Q_SCALE = (MLA_NOPE_DIM + MLA_ROPE_DIM) ** -0.5 * LOG2E
FIXED_SHIFT_ROW_SUM_LIMIT_PER_KEY = 4.0

BF16 = jnp.bfloat16
F32 = jnp.float32
_NT = (((1,), (1,)), ((), ()))


def _params(n_grid_dims):
    return pltpu.CompilerParams(
        dimension_semantics=("arbitrary",) * n_grid_dims,
        vmem_limit_bytes=VMEM_LIMIT_BYTES,
    )


def _rms(x, g):
    var = jnp.mean(x * x, axis=-1, keepdims=True)
    return x * lax.rsqrt(var + NORM_EPS) * g


def _rope_tables(seq, rot_dim, period):
    half = rot_dim // 2
    pos = jnp.arange(seq, dtype=F32)
    inv_freq = ROPE_THETA ** (-jnp.arange(0, rot_dim, 2, dtype=F32) / rot_dim)
    ang = pos[:, None] * inv_freq[None, :]
    cos, sin = jnp.cos(ang), jnp.sin(ang)
    rest = period - rot_dim
    zeros_half = jnp.zeros((seq, half), F32)
    zeros_rest = jnp.zeros((seq, rest), F32)
    a = jnp.concatenate([cos, cos, jnp.ones((seq, rest), F32)], axis=-1)
    b = jnp.concatenate([-sin, zeros_half, zeros_rest], axis=-1)
    c = jnp.concatenate([zeros_half, sin, zeros_rest], axis=-1)
    reps = LANES // period
    return tuple(jnp.tile(t, (1, reps)) for t in (a, b, c))


def _rope_tile(x, a, b, c, half):
    return x * a + pltpu.roll(x, LANES - half, 1) * b + pltpu.roll(x, half, 1) * c


def _rmsnorm_kernel(x_ref, g_ref, o_ref):
    o_ref[...] = _rms(x_ref[...], g_ref[...]).astype(o_ref.dtype)


def _rmsnorm_bf16(x, g, tm=512):
    s, d = x.shape
    return pl.pallas_call(
        _rmsnorm_kernel,
        grid=(s // tm,),
        in_specs=[pl.BlockSpec((tm, d), lambda i: (i, 0)), pl.BlockSpec((1, d), lambda i: (0, 0))],
        out_specs=pl.BlockSpec((tm, d), lambda i: (i, 0)),
        out_shape=jax.ShapeDtypeStruct((s, d), BF16),
        compiler_params=_params(1),
        name="rmsnorm",
    )(x, g.reshape(1, d))


def _proj_kernel(*refs, mode):
    if mode in ("rope_q", "rope_k"):
        h_ref, w_ref, a_ref, b_ref, c_ref, o_ref = refs
    else:
        h_ref, w_ref, o_ref = refs
    acc = jnp.dot(h_ref[...], w_ref[...], preferred_element_type=F32)
    if mode == "plain":
        o_ref[...] = acc.astype(o_ref.dtype)
        return
    if mode == "sigmoid":
        o_ref[...] = jax.nn.sigmoid(acc)
        return
    a, b, c = a_ref[...], b_ref[...], c_ref[...]
    tm, tn = acc.shape
    first_component = lax.broadcasted_iota(jnp.int32, (tm, LANES), 1) < DA_QK_DIM
    for t in range(tn // LANES):
        cols = slice(t * LANES, (t + 1) * LANES)
        y = _rope_tile(acc[:, cols], a, b, c, DA_ROT_DIM // 2)
        if mode == "rope_k":
            o_ref[:, cols] = y.astype(o_ref.dtype)
        else:
            y = y * DA_Q_SCALE
            o_ref[0, :, cols] = jnp.where(first_component, y, 0.0).astype(o_ref.dtype)
            o_ref[1, :, cols] = jnp.where(first_component, 0.0, y).astype(o_ref.dtype)


def _proj(h, w, mode, tables=(), tm=1024, tn=1024):
    s, k = h.shape
    n = w.shape[1]
    in_specs = [pl.BlockSpec((tm, k), lambda i, j: (i, 0)), pl.BlockSpec((k, tn), lambda i, j: (0, j))]
    in_specs += [pl.BlockSpec((tm, LANES), lambda i, j: (i, 0)) for _ in tables]
    if mode == "rope_q":
        out_shape = jax.ShapeDtypeStruct((2, s, n), BF16)
        out_spec = pl.BlockSpec((2, tm, tn), lambda i, j: (0, i, j))
    else:
        out_shape = jax.ShapeDtypeStruct((s, n), F32 if mode == "sigmoid" else BF16)
        out_spec = pl.BlockSpec((tm, tn), lambda i, j: (i, j))
    return pl.pallas_call(
        functools.partial(_proj_kernel, mode=mode),
        grid=(s // tm, n // tn),
        in_specs=in_specs,
        out_specs=out_spec,
        out_shape=out_shape,
        compiler_params=_params(2),
        name="proj_" + mode,
    )(h, w, *tables)


def _mla_pre_kernel(h_ref, win_ref, gq_ref, wqb_ref, gkv_ref, wkvb_ref, a_ref, b_ref, c_ref,
                    q_ref, k_ref, v_ref):
    a, b, c = a_ref[...], b_ref[...], c_ref[...]
    half = MLA_ROPE_DIM // 2
    lat = jnp.dot(h_ref[...], win_ref[...], preferred_element_type=F32)
    cq = _rms(lat[:, :MLA_Q_RANK], gq_ref[...]).astype(BF16)
    ckv = _rms(lat[:, MLA_Q_RANK:MLA_Q_RANK + MLA_KV_RANK], gkv_ref[...]).astype(BF16)
    k_rope = _rope_tile(lat[:, MLA_Q_RANK + MLA_KV_RANK:], a, b, c, half).astype(BF16)
    q = jnp.dot(cq, wqb_ref[...], preferred_element_type=F32)
    kv = jnp.dot(ckv, wkvb_ref[...], preferred_element_type=F32)
    for hd in range(MLA_HEADS):
        lo = hd * MLA_HEAD_BLOCK
        mid = lo + LANES
        hi = lo + MLA_HEAD_BLOCK
        q_ref[:, lo:mid] = (q[:, lo:mid] * MLA_Q_SCALE).astype(BF16)
        q_ref[:, mid:hi] = (_rope_tile(q[:, mid:hi], a, b, c, half) * MLA_Q_SCALE).astype(BF16)
        k_ref[:, lo:mid] = kv[:, lo:mid].astype(BF16)
        k_ref[:, mid:hi] = k_rope
        v_ref[:, hd * LANES:(hd + 1) * LANES] = kv[:, mid:hi].astype(BF16)


def _mla_pre(h, w_mla_in, gq, wqb, gkv, wkvb, tables, tm=256):
    s, d = h.shape
    qk_cols = MLA_HEADS * MLA_HEAD_BLOCK
    v_cols = MLA_HEADS * MLA_V_DIM
    full = lambda arr: pl.BlockSpec(arr.shape, lambda i: (0, 0))
    row = lambda n: pl.BlockSpec((tm, n), lambda i: (i, 0))
    return pl.pallas_call(
        _mla_pre_kernel,
        grid=(s // tm,),
        in_specs=[row(d), full(w_mla_in), full(gq), full(wqb), full(gkv), full(wkvb),
                  row(LANES), row(LANES), row(LANES)],
        out_specs=[row(qk_cols), row(qk_cols), row(v_cols)],
        out_shape=[jax.ShapeDtypeStruct((s, qk_cols), BF16),
                   jax.ShapeDtypeStruct((s, qk_cols), BF16),
                   jax.ShapeDtypeStruct((s, v_cols), BF16)],
        compiler_params=_params(1),
        name="mla_pre",
    )(h, w_mla_in, gq, wqb, gkv, wkvb, *tables)


def _flash_kernel(*refs, mode, tk, lambda_init):
    if mode == "da":
        (q_ref, k_ref, v_ref, lq1_ref, lk1_ref, lq2_ref, lk2_ref, g_ref, o_ref,
         s_sc, p_sc, alpha_sc, m_sc, acc_sc) = refs
    else:
        q_ref, k_ref, v_ref, o_ref, s_sc, p_sc, alpha_sc, m_sc, acc_sc = refs
    rows = m_sc.shape[0]
    n_kv = k_ref.shape[0] // tk
    q = q_ref[...].reshape(rows, q_ref.shape[-1])
    ones = jnp.ones((tk, LANES), BF16)

    def chunk(j):
        return pl.ds(pl.multiple_of(j * tk, tk), tk)

    def scores(j, slot):
        s_sc[slot] = lax.dot_general(q, k_ref[chunk(j), :], _NT, preferred_element_type=F32)

    def weights(slot, shift):
        s = s_sc[slot]
        for t in range(tk // LANES):
            cols = slice(t * LANES, (t + 1) * LANES)
            p_sc[slot, :, cols] = jnp.exp2((s[:, cols] - shift).astype(BF16))

    def p_times_v_and_ones(j, slot):
        v_and_ones = jnp.concatenate([v_ref[chunk(j), :], ones], axis=1)
        return jnp.dot(p_sc[slot], v_and_ones, preferred_element_type=F32)

    def init_fixed():
        acc_sc[...] = jnp.zeros(acc_sc.shape, F32)
        m_sc[...] = jnp.broadcast_to(jnp.max(s_sc[0], axis=-1, keepdims=True), m_sc.shape)

    def softmax_fixed(slot):
        weights(slot, m_sc[...])

    def accumulate_fixed(j, slot):
        acc_sc[...] += p_times_v_and_ones(j, slot)

    def init_online():
        acc_sc[...] = jnp.zeros(acc_sc.shape, F32)
        m_sc[...] = jnp.full(m_sc.shape, -jnp.inf, F32)

    def softmax_online(slot):
        m_prev = m_sc[...]
        m_new = jnp.maximum(m_prev, jnp.max(s_sc[slot], axis=-1, keepdims=True))
        alpha_sc[slot] = jnp.exp2(m_prev - m_new)
        weights(slot, m_new)
        m_sc[...] = m_new

    def accumulate_online(j, slot):
        pv = p_times_v_and_ones(j, slot)
        alpha = alpha_sc[slot]
        acc_sc[:, :LANES] = alpha * acc_sc[:, :LANES] + pv[:, :LANES]
        acc_sc[:, LANES:] = alpha * acc_sc[:, LANES:] + pv[:, LANES:]

    def run_pipeline(init, softmax, accumulate, unroll):
        scores(0, 0)
        scores(1, 1)
        init()
        softmax(0)

        def kv_pair(i, carry):
            t = 2 * i + 2
            scores(t, 0)
            softmax(1)
            accumulate(t - 2, 0)
            scores(t + 1, 1)
            softmax(0)
            accumulate(t - 1, 1)
            return carry

        lax.fori_loop(0, n_kv // 2 - 1, kv_pair, 0, unroll=unroll)
        softmax(1)
        accumulate(n_kv - 2, 0)
        accumulate(n_kv - 1, 1)

    run_pipeline(init_fixed, softmax_fixed, accumulate_fixed, unroll=4)
    row_sum_limit = FIXED_SHIFT_ROW_SUM_LIMIT_PER_KEY * k_ref.shape[0]

    @pl.when(jnp.logical_not(jnp.max(acc_sc[:, LANES:]) <= row_sum_limit))
    def _():
        run_pipeline(init_online, softmax_online, accumulate_online, unroll=1)

    o = acc_sc[:, :LANES] / acc_sc[:, LANES:]
    if mode == "da":
        tq = rows // 2
        lam = (jnp.exp(jnp.sum(lq1_ref[...] * lk1_ref[...])) - jnp.exp(jnp.sum(lq2_ref[...] * lk2_ref[...]))
               + lambda_init)
        diff = o[:tq] - lam * o[tq:]
        o = _rms(diff, g_ref[...]) * (1.0 - lambda_init)
    o_ref[...] = o.astype(o_ref.dtype)


def _flash(q, k, v, mode, extra=(), lambda_init=0.0, tq=512, tk=512):
    s = k.shape[0]
    heads = v.shape[1] // LANES
    dk = k.shape[1] // heads
    if mode == "da":
        rows = 2 * tq
        q_spec = pl.BlockSpec((2, tq, dk), lambda h, i: (0, i, h))
    else:
        rows = tq
        q_spec = pl.BlockSpec((tq, dk), lambda h, i: (i, h))
    in_specs = [q_spec,
                pl.BlockSpec((s, dk), lambda h, i: (0, h)),
                pl.BlockSpec((s, LANES), lambda h, i: (0, h))]
    in_specs += [pl.BlockSpec(e.shape, lambda h, i: (0, 0)) for e in extra]
    return pl.pallas_call(
        functools.partial(_flash_kernel, mode=mode, tk=tk, lambda_init=lambda_init),
        grid=(heads, s // tq),
        in_specs=in_specs,
        out_specs=pl.BlockSpec((tq, LANES), lambda h, i: (i, h)),
        out_shape=jax.ShapeDtypeStruct((s, heads * LANES), BF16),
        scratch_shapes=[pltpu.VMEM((2, rows, tk), F32), pltpu.VMEM((2, rows, tk), BF16),
                        pltpu.VMEM((2, rows, LANES), F32), pltpu.VMEM((rows, LANES), F32),
                        pltpu.VMEM((rows, 2 * LANES), F32)],
        compiler_params=_params(2),
        name="flash_" + mode,
    )(q, k, v, *extra)


def _merge_kernel(oda_ref, omla_ref, gda_ref, gmla_ref, wda_ref, wmla_ref, wo_ref, x_ref, o_ref):
    b_da = jnp.dot(oda_ref[...], wda_ref[...], preferred_element_type=F32)
    b_mla = jnp.dot(omla_ref[...], wmla_ref[...], preferred_element_type=F32)
    merged = gda_ref[...] * b_da + gmla_ref[...] * b_mla
    o_ref[...] = x_ref[...] + jnp.dot(merged.astype(BF16), wo_ref[...], preferred_element_type=F32)


def _merge(o_da, o_mla, gates, w_da, w_mla, w_out, x, tm=256):
    s, d = x.shape
    resident = lambda arr: pl.BlockSpec(arr.shape, lambda i: (0, 0), pipeline_mode=pl.Buffered(1))
    return pl.pallas_call(
        _merge_kernel,
        grid=(s // tm,),
        in_specs=[pl.BlockSpec((tm, o_da.shape[1]), lambda i: (i, 0)),
                  pl.BlockSpec((tm, o_mla.shape[1]), lambda i: (i, 0)),
                  pl.BlockSpec((tm, d), lambda i: (i, 0)),
                  pl.BlockSpec((tm, d), lambda i: (i, 1)),
                  resident(w_da), resident(w_mla), resident(w_out),
                  pl.BlockSpec((tm, d), lambda i: (i, 0))],
        out_specs=pl.BlockSpec((tm, d), lambda i: (i, 0)),
        out_shape=jax.ShapeDtypeStruct((s, d), F32),
        compiler_params=_params(1),
        name="merge",
    )(o_da, o_mla, gates, gates, w_da, w_mla, w_out, x)


def _ffn_kernel(x_ref, g_ref, wg_ref, wu_ref, wd_ref, gf_ref, o_ref, h_sc, acc_sc, *, final_norm):
    f = pl.program_id(1)

    @pl.when(f == 0)
    def _():
        h_sc[...] = _rms(x_ref[...], g_ref[...]).astype(h_sc.dtype)
        acc_sc[...] = jnp.zeros(acc_sc.shape, F32)

    h = h_sc[...]
    gate = jnp.dot(h, wg_ref[...], preferred_element_type=F32)
    up = jnp.dot(h, wu_ref[...], preferred_element_type=F32)
    act = (jax.nn.silu(gate) * up).astype(BF16)
    acc_sc[...] += jnp.dot(act, wd_ref[...], preferred_element_type=F32)

    @pl.when(f == pl.num_programs(1) - 1)
    def _():
        y = x_ref[...] + acc_sc[...]
        o_ref[...] = _rms(y, gf_ref[...]) if final_norm else y


def _ffn(x, g, w_gate, w_up, w_down, g_final, final_norm, tm=512, tf=512):
    s, d = x.shape
    hidden = w_gate.shape[1]
    return pl.pallas_call(
        functools.partial(_ffn_kernel, final_norm=final_norm),
        grid=(s // tm, hidden // tf),
        in_specs=[pl.BlockSpec((tm, d), lambda i, f: (i, 0)),
                  pl.BlockSpec((1, d), lambda i, f: (0, 0)),
                  pl.BlockSpec((d, tf), lambda i, f: (0, f)),
                  pl.BlockSpec((d, tf), lambda i, f: (0, f)),
                  pl.BlockSpec((tf, d), lambda i, f: (f, 0)),
                  pl.BlockSpec((1, d), lambda i, f: (0, 0))],
        out_specs=pl.BlockSpec((tm, d), lambda i, f: (i, 0)),
        out_shape=jax.ShapeDtypeStruct((s, d), F32),
        scratch_shapes=[pltpu.VMEM((tm, d), BF16), pltpu.VMEM((tm, d), F32)],
        compiler_params=_params(2),
        name="ffn",
    )(x, g.reshape(1, d), w_gate, w_up, w_down, g_final.reshape(1, d))


def _prepare_layer(l, attn_norm_g, w_in, da_lambda_q1, da_lambda_k1, da_lambda_q2, da_lambda_k2, da_subln_g,
                   mla_q_norm_g, mla_w_q_b, mla_kv_norm_g, mla_w_kv_b, w_branch_da, w_branch_mla, w_out,
                   ffn_norm_g, w_gate, w_up, w_down):
    w = w_in[l]
    cuts = [0, DA_COLS, 2 * DA_COLS, 3 * DA_COLS, 3 * DA_COLS + MLA_IN_COLS]
    pad = jnp.zeros((D_MODEL, LANES - MLA_ROPE_DIM), F32)
    head_dim = MLA_NOPE_DIM + MLA_ROPE_DIM
    wqb = mla_w_q_b[l].reshape(MLA_Q_RANK, MLA_HEADS, head_dim)
    wqb = jnp.pad(wqb, ((0, 0), (0, 0), (0, MLA_HEAD_BLOCK - head_dim)))
    return dict(
        attn_g=attn_norm_g[l],
        wq=w[:, cuts[0]:cuts[1]].astype(BF16),
        wk=w[:, cuts[1]:cuts[2]].astype(BF16),
        wv=w[:, cuts[2]:cuts[3]].astype(BF16),
        w_mla_in=jnp.concatenate([w[:, cuts[3]:cuts[4]], pad], axis=1).astype(BF16),
        w_gates=w[:, cuts[4]:].astype(BF16),
        lambdas=tuple(p[l].reshape(1, DA_QK_DIM) for p in (da_lambda_q1, da_lambda_k1, da_lambda_q2, da_lambda_k2)),
        subln_g=da_subln_g[l].reshape(1, DA_V_DIM),
        gq=mla_q_norm_g[l].reshape(1, MLA_Q_RANK),
        wqb=wqb.reshape(MLA_Q_RANK, MLA_HEADS * MLA_HEAD_BLOCK).astype(BF16),
        gkv=mla_kv_norm_g[l].reshape(1, MLA_KV_RANK),
        wkvb=mla_w_kv_b[l].astype(BF16),
        w_branch_da=w_branch_da[l].astype(BF16),
        w_branch_mla=w_branch_mla[l].astype(BF16),
        w_out=w_out[l].astype(BF16),
        ffn_g=ffn_norm_g[l],
        w_gate=w_gate[l].astype(BF16),
        w_up=w_up[l].astype(BF16),
        w_down=w_down[l].astype(BF16),
    )


def _layer(x, p, lambda_init, da_tables, mla_tables, final_norm_g, is_last):
    h = _rmsnorm_bf16(x, p["attn_g"])
    q_da = _proj(h, p["wq"], "rope_q", da_tables)
    k_da = _proj(h, p["wk"], "rope_k", da_tables)
    v_da = _proj(h, p["wv"], "plain")
    gates = _proj(h, p["w_gates"], "sigmoid")
    q_mla, k_mla, v_mla = _mla_pre(h, p["w_mla_in"], p["gq"], p["wqb"], p["gkv"], p["wkvb"], mla_tables)
    o_da = _flash(q_da, k_da, v_da, "da", extra=p["lambdas"] + (p["subln_g"],), lambda_init=lambda_init, tq=1024)
    o_mla = _flash(q_mla, k_mla, v_mla, "mla", tq=1024)
    x = _merge(o_da, o_mla, gates, p["w_branch_da"], p["w_branch_mla"], p["w_out"], x)
    return _ffn(x, p["ffn_g"], p["w_gate"], p["w_up"], p["w_down"], final_norm_g, final_norm=is_last)


def kernel(x_prompt, x_sample, attn_norm_g, w_in, da_lambda_q1, da_lambda_k1, da_lambda_q2, da_lambda_k2, da_subln_g, mla_q_norm_g, mla_w_q_b, mla_kv_norm_g, mla_w_kv_b, w_branch_da, w_branch_mla, w_out, ffn_norm_g, w_gate, w_up, w_down, final_norm_g):
    depth = w_in.shape[0]
    layers = [
        _prepare_layer(l, attn_norm_g, w_in, da_lambda_q1, da_lambda_k1, da_lambda_q2, da_lambda_k2, da_subln_g,
                       mla_q_norm_g, mla_w_q_b, mla_kv_norm_g, mla_w_kv_b, w_branch_da, w_branch_mla, w_out,
                       ffn_norm_g, w_gate, w_up, w_down)
        for l in range(depth)
    ]

    max_seq = max(x_prompt.shape[1], x_sample.shape[1])
    da_tables = _rope_tables(max_seq, DA_ROT_DIM, DA_QK_DIM)
    mla_tables = _rope_tables(max_seq, MLA_ROPE_DIM, LANES)

    def trunk(x3):
        batch, seq, d = x3.shape
        x = x3.reshape(batch * seq, d)
        outs = []
        for bi in range(batch):
            xb = x if batch == 1 else lax.slice_in_dim(x, bi * seq, (bi + 1) * seq)
            for l, p in enumerate(layers):
                lambda_init = 0.8 - 0.6 * math.exp(-0.3 * l)
                xb = _layer(xb, p, lambda_init, da_tables, mla_tables, final_norm_g, is_last=(l == depth - 1))
            outs.append(xb)
        out = outs[0] if batch == 1 else jnp.concatenate(outs)
        return out.reshape(batch, seq, d)

    return trunk(x_prompt), trunk(x_sample)
```

```python
import functools
import math

import jax
import jax.numpy as jnp
from jax import lax
from jax.experimental import pallas as pl
from jax.experimental.pallas import tpu as pltpu

D_MODEL = 2048
DA_HEADS = 8
DA_QK_DIM = 64
DA_V_DIM = 2 * DA_QK_DIM
DA_ROT_DIM = DA_QK_DIM // 4
MLA_HEADS = 8
MLA_Q_RANK = 512
MLA_KV_RANK = 512
MLA_NOPE_DIM = 128
MLA_ROPE_DIM = 64
MLA_V_DIM = 128
ROPE_THETA = 500000.0
NORM_EPS = 1e-6
FFN_HIDDEN = -(-8 * D_MODEL // (3 * 256)) * 256
DA_COLS = DA_HEADS * 2 * DA_QK_DIM
MLA_IN_COLS = MLA_Q_RANK + MLA_KV_RANK + MLA_ROPE_DIM
GATE_COLS = 2 * D_MODEL

LANES = 128
MLA_HEAD_BLOCK = 2 * LANES
VMEM_LIMIT_BYTES = 56 * 1024 * 1024
LOG2E = math.log2(math.e)
DA_Q_SCALE = DA_QK_DIM ** -0.5 * LOG2E
MLA_Q_SCALE = (MLA_NOPE_DIM + MLA_ROPE_DIM) ** -0.5 * LOG2E
FIXED_SHIFT_ROW_SUM_LIMIT_PER_KEY = 4.0

BF16 = jnp.bfloat16
F32 = jnp.float32
_NT = (((1,), (1,)), ((), ()))


def _params(n_grid_dims):
    return pltpu.CompilerParams(
        dimension_semantics=("arbitrary",) * n_grid_dims,
        vmem_limit_bytes=VMEM_LIMIT_BYTES,
    )


def _rms(x, g):
    var = jnp.mean(x * x, axis=-1, keepdims=True)
    return x * lax.rsqrt(var + NORM_EPS) * g


def _rope_tables(seq, rot_dim, period):
    half = rot_dim // 2
    pos = jnp.arange(seq, dtype=F32)
    inv_freq = ROPE_THETA ** (-jnp.arange(0, rot_dim, 2, dtype=F32) / rot_dim)
    ang = pos[:, None] * inv_freq[None, :]
    cos, sin = jnp.cos(ang), jnp.sin(ang)
    rest = period - rot_dim
    zeros_half = jnp.zeros((seq, half), F32)
    zeros_rest = jnp.zeros((seq, rest), F32)
    a = jnp.concatenate([cos, cos, jnp.ones((seq, rest), F32)], axis=-1)
    b = jnp.concatenate([-sin, zeros_half, zeros_rest], axis=-1)
    c = jnp.concatenate([zeros_half, sin, zeros_rest], axis=-1)
    reps = LANES // period
    return tuple(jnp.tile(t, (1, reps)) for t in (a, b, c))


def _rope_tile(x, a, b, c, half):
    return x * a + pltpu.roll(x, LANES - half, 1) * b + pltpu.roll(x, half, 1) * c


def _rmsnorm_kernel(x_ref, g_ref, o_ref):
    o_ref[...] = _rms(x_ref[...], g_ref[...]).astype(o_ref.dtype)


def _rmsnorm_bf16(x, g, tm=512):
    s, d = x.shape
    return pl.pallas_call(
        _rmsnorm_kernel,
        grid=(s // tm,),
        in_specs=[pl.BlockSpec((tm, d), lambda i: (i, 0)), pl.BlockSpec((1, d), lambda i: (0, 0))],
        out_specs=pl.BlockSpec((tm, d), lambda i: (i, 0)),
        out_shape=jax.ShapeDtypeStruct((s, d), BF16),
        compiler_params=_params(1),
        name="rmsnorm",
    )(x, g.reshape(1, d))


def _proj_kernel(*refs, mode):
    if mode in ("rope_q", "rope_k"):
        h_ref, w_ref, a_ref, b_ref, c_ref, o_ref = refs
    else:
        h_ref, w_ref, o_ref = refs
    acc = jnp.dot(h_ref[...], w_ref[...], preferred_element_type=F32)
    if mode == "plain":
        o_ref[...] = acc.astype(o_ref.dtype)
        return
    a, b, c = a_ref[...], b_ref[...], c_ref[...]
    tm, tn = acc.shape
    first_component = lax.broadcasted_iota(jnp.int32, (tm, LANES), 1) < DA_QK_DIM
    for t in range(tn // LANES):
        cols = slice(t * LANES, (t + 1) * LANES)
        y = _rope_tile(acc[:, cols], a, b, c, DA_ROT_DIM // 2)
        if mode == "rope_k":
            o_ref[:, cols] = y.astype(o_ref.dtype)
        else:
            y = y * DA_Q_SCALE
            o_ref[0, :, cols] = jnp.where(first_component, y, 0.0).astype(o_ref.dtype)
            o_ref[1, :, cols] = jnp.where(first_component, 0.0, y).astype(o_ref.dtype)


def _proj(h, w, mode, tables=(), tm=1024, tn=1024):
    s, k = h.shape
    n = w.shape[1]
    in_specs = [pl.BlockSpec((tm, k), lambda i, j: (i, 0)), pl.BlockSpec((k, tn), lambda i, j: (0, j))]
    in_specs += [pl.BlockSpec((tm, LANES), lambda i, j: (i, 0)) for _ in tables]
    if mode == "rope_q":
        out_shape = jax.ShapeDtypeStruct((2, s, n), BF16)
        out_spec = pl.BlockSpec((2, tm, tn), lambda i, j: (0, i, j))
    else:
        out_shape = jax.ShapeDtypeStruct((s, n), BF16)
        out_spec = pl.BlockSpec((tm, tn), lambda i, j: (i, j))
    return pl.pallas_call(
        functools.partial(_proj_kernel, mode=mode),
        grid=(s // tm, n // tn),
        in_specs=in_specs,
        out_specs=out_spec,
        out_shape=out_shape,
        compiler_params=_params(2),
        name="proj_" + mode,
    )(h, w, *tables)


def _mla_pre_kernel(h_ref, win_ref, gq_ref, wqb_ref, gkv_ref, wkvb_ref, a_ref, b_ref, c_ref,
                    q_ref, k_ref, v_ref):
    a, b, c = a_ref[...], b_ref[...], c_ref[...]
    half = MLA_ROPE_DIM // 2
    lat = jnp.dot(h_ref[...], win_ref[...], preferred_element_type=F32)
    cq = _rms(lat[:, :MLA_Q_RANK], gq_ref[...]).astype(BF16)
    ckv = _rms(lat[:, MLA_Q_RANK:MLA_Q_RANK + MLA_KV_RANK], gkv_ref[...]).astype(BF16)
    k_rope = _rope_tile(lat[:, MLA_Q_RANK + MLA_KV_RANK:], a, b, c, half).astype(BF16)
    q = jnp.dot(cq, wqb_ref[...], preferred_element_type=F32)
    kv = jnp.dot(ckv, wkvb_ref[...], preferred_element_type=F32)
    for hd in range(MLA_HEADS):
        lo = hd * MLA_HEAD_BLOCK
        mid = lo + LANES
        hi = lo + MLA_HEAD_BLOCK
        q_ref[:, lo:mid] = (q[:, lo:mid] * MLA_Q_SCALE).astype(BF16)
        q_ref[:, mid:hi] = (_rope_tile(q[:, mid:hi], a, b, c, half) * MLA_Q_SCALE).astype(BF16)
        k_ref[:, lo:mid] = kv[:, lo:mid].astype(BF16)
        k_ref[:, mid:hi] = k_rope
        v_ref[:, hd * LANES:(hd + 1) * LANES] = kv[:, mid:hi].astype(BF16)


def _mla_pre(h, w_mla_in, gq, wqb, gkv, wkvb, tables, tm=256):
    s, d = h.shape
    qk_cols = MLA_HEADS * MLA_HEAD_BLOCK
    v_cols = MLA_HEADS * MLA_V_DIM
    full = lambda arr: pl.BlockSpec(arr.shape, lambda i: (0, 0))
    row = lambda n: pl.BlockSpec((tm, n), lambda i: (i, 0))
    return pl.pallas_call(
        _mla_pre_kernel,
        grid=(s // tm,),
        in_specs=[row(d), full(w_mla_in), full(gq), full(wqb), full(gkv), full(wkvb),
                  row(LANES), row(LANES), row(LANES)],
        out_specs=[row(qk_cols), row(qk_cols), row(v_cols)],
        out_shape=[jax.ShapeDtypeStruct((s, qk_cols), BF16),
                   jax.ShapeDtypeStruct((s, qk_cols), BF16),
                   jax.ShapeDtypeStruct((s, v_cols), BF16)],
        compiler_params=_params(1),
        name="mla_pre",
    )(h, w_mla_in, gq, wqb, gkv, wkvb, *tables)


def _flash_kernel(*refs, mode, tk, lambda_init):
    if mode == "da":
        (q_ref, k_ref, v_ref, lq1_ref, lk1_ref, lq2_ref, lk2_ref, g_ref, o_ref,
         s_sc, p_sc, alpha_sc, m_sc, acc_sc) = refs
    else:
        q_ref, k_ref, v_ref, o_ref, s_sc, p_sc, alpha_sc, m_sc, acc_sc = refs
    rows = m_sc.shape[0]
    n_kv = k_ref.shape[0] // tk
    q = q_ref[...].reshape(rows, q_ref.shape[-1])
    ones = jnp.ones((tk, LANES), BF16)

    def chunk(j):
        return pl.ds(pl.multiple_of(j * tk, tk), tk)

    def scores(j, slot):
        s_sc[slot] = lax.dot_general(q, k_ref[chunk(j), :], _NT, preferred_element_type=F32)

    def weights(slot, shift):
        s = s_sc[slot]
        for t in range(tk // LANES):
            cols = slice(t * LANES, (t + 1) * LANES)
            p_sc[slot, :, cols] = jnp.exp2((s[:, cols] - shift).astype(BF16))

    def p_times_v_and_ones(j, slot):
        v_and_ones = jnp.concatenate([v_ref[chunk(j), :], ones], axis=1)
        return jnp.dot(p_sc[slot], v_and_ones, preferred_element_type=F32)

    def init_fixed():
        acc_sc[...] = jnp.zeros(acc_sc.shape, F32)
        m_sc[...] = jnp.broadcast_to(jnp.max(s_sc[0], axis=-1, keepdims=True), m_sc.shape)

    def softmax_fixed(slot):
        weights(slot, m_sc[...])

    def accumulate_fixed(j, slot):
        acc_sc[...] += p_times_v_and_ones(j, slot)

    def init_online():
        acc_sc[...] = jnp.zeros(acc_sc.shape, F32)
        m_sc[...] = jnp.full(m_sc.shape, -jnp.inf, F32)

    def softmax_online(slot):
        m_prev = m_sc[...]
        m_new = jnp.maximum(m_prev, jnp.max(s_sc[slot], axis=-1, keepdims=True))
        alpha_sc[slot] = jnp.exp2(m_prev - m_new)
        weights(slot, m_new)
        m_sc[...] = m_new

    def accumulate_online(j, slot):
        pv = p_times_v_and_ones(j, slot)
        alpha = alpha_sc[slot]
        acc_sc[:, :LANES] = alpha * acc_sc[:, :LANES] + pv[:, :LANES]
        acc_sc[:, LANES:] = alpha * acc_sc[:, LANES:] + pv[:, LANES:]

    def run_pipeline(init, softmax, accumulate, unroll):
        scores(0, 0)
        scores(1, 1)
        init()
        softmax(0)

        def kv_pair(i, carry):
            t = 2 * i + 2
            scores(t, 0)
            softmax(1)
            accumulate(t - 2, 0)
            scores(t + 1, 1)
            softmax(0)
            accumulate(t - 1, 1)
            return carry

        lax.fori_loop(0, n_kv // 2 - 1, kv_pair, 0, unroll=unroll)
        softmax(1)
        accumulate(n_kv - 2, 0)
        accumulate(n_kv - 1, 1)

    run_pipeline(init_fixed, softmax_fixed, accumulate_fixed, unroll=4)
    row_sum_limit = FIXED_SHIFT_ROW_SUM_LIMIT_PER_KEY * k_ref.shape[0]

    @pl.when(jnp.logical_not(jnp.max(acc_sc[:, LANES:]) <= row_sum_limit))
    def _():
        run_pipeline(init_online, softmax_online, accumulate_online, unroll=1)

    o = acc_sc[:, :LANES] / acc_sc[:, LANES:]
    if mode == "da":
        tq = rows // 2
        lam = (jnp.exp(jnp.sum(lq1_ref[...] * lk1_ref[...])) - jnp.exp(jnp.sum(lq2_ref[...] * lk2_ref[...]))
               + lambda_init)
        diff = o[:tq] - lam * o[tq:]
        o = _rms(diff, g_ref[...]) * (1.0 - lambda_init)
    o_ref[...] = o.astype(o_ref.dtype)


def _flash(q, k, v, mode, extra=(), lambda_init=0.0, tq=512, tk=512):
    s = k.shape[0]
    heads = v.shape[1] // LANES
    dk = k.shape[1] // heads
    if mode == "da":
        rows = 2 * tq
        q_spec = pl.BlockSpec((2, tq, dk), lambda h, i: (0, i, h))
    else:
        rows = tq
        q_spec = pl.BlockSpec((tq, dk), lambda h, i: (i, h))
    in_specs = [q_spec,
                pl.BlockSpec((s, dk), lambda h, i: (0, h)),
                pl.BlockSpec((s, LANES), lambda h, i: (0, h))]
    in_specs += [pl.BlockSpec(e.shape, lambda h, i: (0, 0)) for e in extra]
    return pl.pallas_call(
        functools.partial(_flash_kernel, mode=mode, tk=tk, lambda_init=lambda_init),
        grid=(heads, s // tq),
        in_specs=in_specs,
        out_specs=pl.BlockSpec((tq, LANES), lambda h, i: (i, h)),
        out_shape=jax.ShapeDtypeStruct((s, heads * LANES), BF16),
        scratch_shapes=[pltpu.VMEM((2, rows, tk), F32), pltpu.VMEM((2, rows, tk), BF16),
                        pltpu.VMEM((2, rows, LANES), F32), pltpu.VMEM((rows, LANES), F32),
                        pltpu.VMEM((rows, 2 * LANES), F32)],
        compiler_params=_params(2),
        name="flash_" + mode,
    )(q, k, v, *extra)


def _merge_kernel(h_ref, oda_ref, omla_ref, wg_ref, wda_ref, wmla_ref, wo_ref, x_ref, o_ref):
    d = x_ref.shape[1]
    h = h_ref[...]
    g_da = jax.nn.sigmoid(jnp.dot(h, wg_ref[:, :d], preferred_element_type=F32))
    merged = g_da * jnp.dot(oda_ref[...], wda_ref[...], preferred_element_type=F32)
    g_mla = jax.nn.sigmoid(jnp.dot(h, wg_ref[:, d:], preferred_element_type=F32))
    merged = merged + g_mla * jnp.dot(omla_ref[...], wmla_ref[...], preferred_element_type=F32)
    o_ref[...] = x_ref[...] + jnp.dot(merged.astype(BF16), wo_ref[...], preferred_element_type=F32)


def _merge(h, o_da, o_mla, w_gates, w_da, w_mla, w_out, x, tm=256):
    s, d = x.shape
    resident = lambda arr: pl.BlockSpec(arr.shape, lambda i: (0, 0), pipeline_mode=pl.Buffered(1))
    rows = lambda arr: pl.BlockSpec((tm, arr.shape[1]), lambda i: (i, 0))
    return pl.pallas_call(
        _merge_kernel,
        grid=(s // tm,),
        in_specs=[rows(h), rows(o_da), rows(o_mla),
                  resident(w_gates), resident(w_da), resident(w_mla), resident(w_out), rows(x)],
        out_specs=rows(x),
        out_shape=jax.ShapeDtypeStruct((s, d), F32),
        compiler_params=_params(1),
        name="merge",
    )(h, o_da, o_mla, w_gates, w_da, w_mla, w_out, x)


def _ffn_kernel(x_ref, g_ref, wg_ref, wu_ref, wd_ref, gf_ref, o_ref, h_sc, acc_sc, *, final_norm):
    f = pl.program_id(1)

    @pl.when(f == 0)
    def _():
        h_sc[...] = _rms(x_ref[...], g_ref[...]).astype(h_sc.dtype)
        acc_sc[...] = jnp.zeros(acc_sc.shape, F32)

    h = h_sc[...]
    gate = jnp.dot(h, wg_ref[...], preferred_element_type=F32)
    up = jnp.dot(h, wu_ref[...], preferred_element_type=F32)
    act = (jax.nn.silu(gate) * up).astype(BF16)
    acc_sc[...] += jnp.dot(act, wd_ref[...], preferred_element_type=F32)

    @pl.when(f == pl.num_programs(1) - 1)
    def _():
        y = x_ref[...] + acc_sc[...]
        o_ref[...] = _rms(y, gf_ref[...]) if final_norm else y


def _ffn(x, g, w_gate, w_up, w_down, g_final, final_norm, tm=512, tf=512):
    s, d = x.shape
    hidden = w_gate.shape[1]
    return pl.pallas_call(
        functools.partial(_ffn_kernel, final_norm=final_norm),
        grid=(s // tm, hidden // tf),
        in_specs=[pl.BlockSpec((tm, d), lambda i, f: (i, 0)),
                  pl.BlockSpec((1, d), lambda i, f: (0, 0)),
                  pl.BlockSpec((d, tf), lambda i, f: (0, f)),
                  pl.BlockSpec((d, tf), lambda i, f: (0, f)),
                  pl.BlockSpec((tf, d), lambda i, f: (f, 0)),
                  pl.BlockSpec((1, d), lambda i, f: (0, 0))],
        out_specs=pl.BlockSpec((tm, d), lambda i, f: (i, 0)),
        out_shape=jax.ShapeDtypeStruct((s, d), F32),
        scratch_shapes=[pltpu.VMEM((tm, d), BF16), pltpu.VMEM((tm, d), F32)],
        compiler_params=_params(2),
        name="ffn",
    )(x, g.reshape(1, d), w_gate, w_up, w_down, g_final.reshape(1, d))


def _prepare_layer(l, attn_norm_g, w_in, da_lambda_q1, da_lambda_k1, da_lambda_q2, da_lambda_k2, da_subln_g,
                   mla_q_norm_g, mla_w_q_b, mla_kv_norm_g, mla_w_kv_b, w_branch_da, w_branch_mla, w_out,
                   ffn_norm_g, w_gate, w_up, w_down):
    w = w_in[l]
    cuts = [0, DA_COLS, 2 * DA_COLS, 3 * DA_COLS, 3 * DA_COLS + MLA_IN_COLS]
    pad = jnp.zeros((D_MODEL, LANES - MLA_ROPE_DIM), F32)
    head_dim = MLA_NOPE_DIM + MLA_ROPE_DIM
    wqb = mla_w_q_b[l].reshape(MLA_Q_RANK, MLA_HEADS, head_dim)
    wqb = jnp.pad(wqb, ((0, 0), (0, 0), (0, MLA_HEAD_BLOCK - head_dim)))
    return dict(
        attn_g=attn_norm_g[l],
        wq=w[:, cuts[0]:cuts[1]].astype(BF16),
        wk=w[:, cuts[1]:cuts[2]].astype(BF16),
        wv=w[:, cuts[2]:cuts[3]].astype(BF16),
        w_mla_in=jnp.concatenate([w[:, cuts[3]:cuts[4]], pad], axis=1).astype(BF16),
        w_gates=w[:, cuts[4]:].astype(BF16),
        lambdas=tuple(p[l].reshape(1, DA_QK_DIM) for p in (da_lambda_q1, da_lambda_k1, da_lambda_q2, da_lambda_k2)),
        subln_g=da_subln_g[l].reshape(1, DA_V_DIM),
        gq=mla_q_norm_g[l].reshape(1, MLA_Q_RANK),
        wqb=wqb.reshape(MLA_Q_RANK, MLA_HEADS * MLA_HEAD_BLOCK).astype(BF16),
        gkv=mla_kv_norm_g[l].reshape(1, MLA_KV_RANK),
        wkvb=mla_w_kv_b[l].astype(BF16),
        w_branch_da=w_branch_da[l].astype(BF16),
        w_branch_mla=w_branch_mla[l].astype(BF16),
        w_out=w_out[l].astype(BF16),
        ffn_g=ffn_norm_g[l],
        w_gate=w_gate[l].astype(BF16),
        w_up=w_up[l].astype(BF16),
        w_down=w_down[l].astype(BF16),
    )


def _layer(x, p, lambda_init, da_tables, mla_tables, final_norm_g, is_last):
    h = _rmsnorm_bf16(x, p["attn_g"])
    q_da = _proj(h, p["wq"], "rope_q", da_tables)
    k_da = _proj(h, p["wk"], "rope_k", da_tables)
    v_da = _proj(h, p["wv"], "plain")
    q_mla, k_mla, v_mla = _mla_pre(h, p["w_mla_in"], p["gq"], p["wqb"], p["gkv"], p["wkvb"], mla_tables)
    o_da = _flash(q_da, k_da, v_da, "da", extra=p["lambdas"] + (p["subln_g"],), lambda_init=lambda_init, tq=1024)
    o_mla = _flash(q_mla, k_mla, v_mla, "mla", tq=1024)
    x = _merge(h, o_da, o_mla, p["w_gates"], p["w_branch_da"], p["w_branch_mla"], p["w_out"], x)
    return _ffn(x, p["ffn_g"], p["w_gate"], p["w_up"], p["w_down"], final_norm_g, final_norm=is_last)


def kernel(x_prompt, x_sample, attn_norm_g, w_in, da_lambda_q1, da_lambda_k1, da_lambda_q2, da_lambda_k2, da_subln_g, mla_q_norm_g, mla_w_q_b, mla_kv_norm_g, mla_w_kv_b, w_branch_da, w_branch_mla, w_out, ffn_norm_g, w_gate, w_up, w_down, final_norm_g):
    depth = w_in.shape[0]
    layers = [
        _prepare_layer(l, attn_norm_g, w_in, da_lambda_q1, da_lambda_k1, da_lambda_q2, da_lambda_k2, da_subln_g,
                       mla_q_norm_g, mla_w_q_b, mla_kv_norm_g, mla_w_kv_b, w_branch_da, w_branch_mla, w_out,
                       ffn_norm_g, w_gate, w_up, w_down)
        for l in range(depth)
    ]

    max_seq = max(x_prompt.shape[1], x_sample.shape[1])
    da_tables = _rope_tables(max_seq, DA_ROT_DIM, DA_QK_DIM)
    mla_tables = _rope_tables(max_seq, MLA_ROPE_DIM, LANES)

    def trunk(x3):
        batch, seq, d = x3.shape
        x = x3.reshape(batch * seq, d)
        outs = []
        for bi in range(batch):
            xb = x if batch == 1 else lax.slice_in_dim(x, bi * seq, (bi + 1) * seq)
            for l, p in enumerate(layers):
                lambda_init = 0.8 - 0.6 * math.exp(-0.3 * l)
                xb = _layer(xb, p, lambda_init, da_tables, mla_tables, final_norm_g, is_last=(l == depth - 1))
            outs.append(xb)
        out = outs[0] if batch == 1 else jnp.concatenate(outs)
        return out.reshape(batch, seq, d)

    return trunk(x_prompt), trunk(x_sample)
```
